```python
import numpy as np
import jax, jax.numpy as jnp
from jax import lax

D_MODEL = 1024
BATCH = 4
SEQ = 8192
DEPTH = 1

D_CONV = D_MODEL
CONV_K = 3
HEAD_DIM = 64
N_HEADS = D_MODEL // HEAD_DIM
N_KV_HEADS = 4
GROUP = N_HEADS // N_KV_HEADS
D_ATTN = N_HEADS * HEAD_DIM
D_KV = N_KV_HEADS * HEAD_DIM
WINDOW = 128
BLOCK = 128
ROPE_THETA = 10000.0
D_FF = 2816
N_MOD = 9
EPS = 1e-6
ADA_INIT = 0.5
NEG_INF = -1e30

IN_SPLITS = [D_CONV, D_CONV, D_CONV, D_ATTN, D_KV, D_KV, D_MODEL, D_MODEL]
IN_OFFSETS = [int(o) for o in np.cumsum(IN_SPLITS)[:-1]]
D_IN = int(sum(IN_SPLITS))

kernel_name = "macaron_conv_swa_sink_hybrid"


def rms_norm(x, g):
    xf = x.astype(jnp.float32)
    y = xf * lax.rsqrt(jnp.mean(xf * xf, axis=-1, keepdims=True) + EPS)
    return (y * g.astype(jnp.float32)).astype(x.dtype)


def modulate(h, shift, scale):
    return h * (1 + scale[:, None, :]) + shift[:, None, :]


def swiglu(h, w_gu, w_down):
    a, b = jnp.split(h @ w_gu, 2, axis=-1)
    return (jax.nn.silu(a) * b) @ w_down


def rope_tables(seq):
    inv = 1.0 / (ROPE_THETA ** (jnp.arange(0, HEAD_DIM, 2, dtype=jnp.float32) / HEAD_DIM))
    ang = jnp.arange(seq, dtype=jnp.float32)[:, None] * inv[None, :]
    return jnp.cos(ang), jnp.sin(ang)


def apply_rope(t, cos, sin):
    t1, t2 = jnp.split(t.astype(jnp.float32), 2, axis=-1)
    c = cos[None, :, None, :]
    s = sin[None, :, None, :]
    return jnp.concatenate([t1 * c - t2 * s, t2 * c + t1 * s], axis=-1).astype(t.dtype)


def short_conv(u, w):
    s = u.shape[1]
    up = jnp.pad(u, ((0, 0), (CONV_K - 1, 0), (0, 0)))
    out = up[:, 0:s] * w[0]
    for j in range(1, CONV_K):
        out = out + up[:, j:j + s] * w[j]
    return out


def sliding_window_attention(q, k, v, sinks):
    bsz, s, _, _ = q.shape
    nb = s // BLOCK
    qb = q.reshape(bsz, nb, BLOCK, N_KV_HEADS, GROUP, HEAD_DIM)

    def band(t):
        tp = jnp.pad(t, ((0, 0), (BLOCK, 0), (0, 0), (0, 0)))
        tp = tp.reshape(bsz, nb + 1, BLOCK, N_KV_HEADS, HEAD_DIM)
        return jnp.concatenate([tp[:, :-1], tp[:, 1:]], axis=2)

    kb, vb = band(k), band(v)
    scores = jnp.einsum('bnqhgd,bnkhd->bnhgqk', qb, kb,
                        preferred_element_type=jnp.float32) * (HEAD_DIM ** -0.5)
    qi = jnp.arange(BLOCK)[:, None]
    kj = jnp.arange(2 * BLOCK)[None, :]
    diff = BLOCK + qi - kj
    in_window = (diff >= 0) & (diff < WINDOW)
    k_pos = (jnp.arange(nb)[:, None] - 1) * BLOCK + jnp.arange(2 * BLOCK)[None, :]
    valid = in_window[None] & (k_pos >= 0)[:, None, :]
    scores = jnp.where(valid[None, :, None, None], scores, NEG_INF)
    sink = sinks.astype(jnp.float32).reshape(N_KV_HEADS, GROUP)[None, None, :, :, None, None]
    m = jnp.maximum(jnp.max(scores, axis=-1, keepdims=True), sink)
    p = jnp.exp(scores - m)
    denom = jnp.sum(p, axis=-1, keepdims=True) + jnp.exp(sink - m)
    probs = (p / denom).astype(v.dtype)
    out = jnp.einsum('bnhgqk,bnkhd->bnqhgd', probs, vb)
    return out.reshape(bsz, s, D_ATTN)


def setup_inputs(seed: int = 0) -> dict:
    key = jax.random.key(seed)
    ks = jax.random.split(key, 24)
    f32 = jnp.float32

    def nrm(k, shape, fan_in, mult=1.0):
        return jax.random.normal(k, shape, f32) * (mult * fan_in ** -0.5)

    def gain(k, shape):
        return 1.0 + 0.05 * jax.random.normal(k, shape, f32)

    L = DEPTH
    return {
        "x": jax.random.normal(ks[0], (BATCH, SEQ, D_MODEL), f32),
        "c": jax.random.normal(ks[1], (BATCH, D_MODEL), f32),
        "w_ada": nrm(ks[2], (L, D_MODEL, N_MOD * D_MODEL), D_MODEL, ADA_INIT),
        "b_ada": 0.02 * jax.random.normal(ks[3], (L, N_MOD * D_MODEL), f32),
        "g_ffn1": gain(ks[4], (L, D_MODEL)),
        "w1_gu": nrm(ks[5], (L, D_MODEL, 2 * D_FF), D_MODEL),
        "w1_down": nrm(ks[6], (L, D_FF, D_MODEL), D_FF),
        "g_mix": gain(ks[7], (L, D_MODEL)),
        "w_in": nrm(ks[8], (L, D_MODEL, D_IN), D_MODEL),
        "conv_w": nrm(ks[9], (L, CONV_K, D_CONV), CONV_K),
        "w_conv_proj": nrm(ks[10], (L, D_CONV, D_MODEL), D_CONV),
        "w_attn_proj": nrm(ks[11], (L, D_ATTN, D_MODEL), D_ATTN),
        "sinks": jax.random.normal(ks[12], (L, N_HEADS), f32),
        "w_out": nrm(ks[13], (L, D_MODEL, D_MODEL), D_MODEL),
        "g_ffn2": gain(ks[14], (L, D_MODEL)),
        "w2_gu": nrm(ks[15], (L, D_MODEL, 2 * D_FF), D_MODEL),
        "w2_down": nrm(ks[16], (L, D_FF, D_MODEL), D_FF),
        "g_final": gain(ks[17], (D_MODEL,)),
    }


def reference(x, c, w_ada, b_ada, g_ffn1, w1_gu, w1_down, g_mix, w_in, conv_w,
              w_conv_proj, w_attn_proj, sinks, w_out, g_ffn2, w2_gu, w2_down, g_final):
    bsz, s, _ = x.shape
    cos, sin = rope_tables(s)
    c_act = jax.nn.silu(c)
    for l in range(DEPTH):
        mods = jnp.split(c_act @ w_ada[l] + b_ada[l], N_MOD, axis=-1)
        sh1, sc1, gt1, sh2, sc2, gt2, sh3, sc3, gt3 = mods

        h = modulate(rms_norm(x, g_ffn1[l]), sh1, sc1)
        x = x + 0.5 * gt1[:, None, :] * swiglu(h, w1_gu[l], w1_down[l])

        h = modulate(rms_norm(x, g_mix[l]), sh2, sc2)
        proj = h @ w_in[l]
        b_g, c_g, u, q, k, v, z_conv, z_attn = jnp.split(proj, IN_OFFSETS, axis=-1)

        y_conv = (b_g * short_conv(c_g * u, conv_w[l])) @ w_conv_proj[l]

        q = apply_rope(q.reshape(bsz, s, N_HEADS, HEAD_DIM), cos, sin)
        k = apply_rope(k.reshape(bsz, s, N_KV_HEADS, HEAD_DIM), cos, sin)
        v = v.reshape(bsz, s, N_KV_HEADS, HEAD_DIM)
        y_attn = sliding_window_attention(q, k, v, sinks[l]) @ w_attn_proj[l]

        merged = jax.nn.sigmoid(z_conv) * y_conv + jax.nn.sigmoid(z_attn) * y_attn
        x = x + gt2[:, None, :] * (merged @ w_out[l])

        h = modulate(rms_norm(x, g_ffn2[l]), sh3, sc3)
        x = x + 0.5 * gt3[:, None, :] * swiglu(h, w2_gu[l], w2_down[l])

    return rms_norm(x, g_final)
```

```python
import functools

import numpy as np
import jax
import jax.numpy as jnp
from jax import lax
from jax.experimental import pallas as pl
from jax.experimental.pallas import tpu as pltpu

D_MODEL = 1024
CONV_K = 3
HEAD_DIM = 64
HALF = HEAD_DIM // 2
N_HEADS = 16
N_KV_HEADS = 4
GROUP = N_HEADS // N_KV_HEADS
D_ATTN = N_HEADS * HEAD_DIM
D_KV = N_KV_HEADS * HEAD_DIM
WINDOW = 128
BLOCK = 128
ROPE_THETA = 10000.0
D_FF = 2816
N_MOD = 9
EPS = 1e-6
NEG_INF = -1e30

LANES = 128
SUBLANES = 8
VMEM_LIMIT_BYTES = 56 * 1024 * 1024

TM_FFN = 256
TM_MIX = 256
TN_ADA = 1024

OFF_B, OFF_C, OFF_U = 0, D_MODEL, 2 * D_MODEL
OFF_Q = 3 * D_MODEL
OFF_K = OFF_Q + D_ATTN
OFF_V = OFF_K + D_KV
OFF_ZC = OFF_V + D_KV
OFF_ZA = OFF_ZC + D_MODEL
D_IN = OFF_ZA + D_MODEL

BF16 = jnp.bfloat16
F32 = jnp.float32


def _const_spec(shape):
    nd = len(shape)
    return pl.BlockSpec(shape, lambda *_: (0,) * nd, pipeline_mode=pl.Buffered(1))


def _rms_mod(x, g, shift, scale):
    ms = jnp.mean(x * x, axis=-1, keepdims=True)
    y = x * lax.rsqrt(ms + EPS) * g
    return y * (1.0 + scale) + shift


def _ada_kernel(c_ref, w_ref, b_ref, o_ref):
    c = c_ref[...]
    ca = (c * jax.nn.sigmoid(c)).astype(BF16)
    o_ref[...] = jnp.dot(ca, w_ref[...].astype(BF16),
                         preferred_element_type=F32) + b_ref[...]


def _ada(c_pad, w_ada, b_ada):
    rows = c_pad.shape[0]
    n = w_ada.shape[1]
    return pl.pallas_call(
        _ada_kernel,
        grid=(n // TN_ADA,),
        in_specs=[
            pl.BlockSpec((rows, D_MODEL), lambda j: (0, 0)),
            pl.BlockSpec((D_MODEL, TN_ADA), lambda j: (0, j)),
            pl.BlockSpec((1, TN_ADA), lambda j: (0, j)),
        ],
        out_specs=pl.BlockSpec((rows, TN_ADA), lambda j: (0, j)),
        out_shape=jax.ShapeDtypeStruct((rows, n), F32),
        compiler_params=pltpu.CompilerParams(
            dimension_semantics=("arbitrary",),
            vmem_limit_bytes=VMEM_LIMIT_BYTES),
        name="ada_mod",
    )(c_pad, w_ada, b_ada)


def _ffn_kernel(x_ref, mod_ref, g_ref, wgu_ref, wd_ref, gf_ref, o_ref, *, mod_base, final):
    x = x_ref[...]
    shift = mod_ref[mod_base:mod_base + 1, :]
    scale = mod_ref[mod_base + 1:mod_base + 2, :]
    gate = mod_ref[mod_base + 2:mod_base + 3, :]
    h = _rms_mod(x, g_ref[...], shift, scale).astype(BF16)
    gu = jnp.dot(h, wgu_ref[...], preferred_element_type=F32)
    a = gu[:, :D_FF]
    b = gu[:, D_FF:]
    act = (a * jax.nn.sigmoid(a) * b).astype(BF16)
    f = jnp.dot(act, wd_ref[...], preferred_element_type=F32)
    xn = x + (0.5 * gate) * f
    if final:
        ms = jnp.mean(xn * xn, axis=-1, keepdims=True)
        xn = xn * lax.rsqrt(ms + EPS) * gf_ref[...]
    o_ref[...] = xn


def _ffn(x, mods, g, wgu, wd, g_final, *, mod_base, final):
    bsz, seq, d = x.shape
    kern = functools.partial(_ffn_kernel, mod_base=mod_base, final=final)
    return pl.pallas_call(
        kern,
        grid=(bsz, seq // TM_FFN),
        in_specs=[
            pl.BlockSpec((None, TM_FFN, d), lambda b, s: (b, s, 0)),
            pl.BlockSpec((None, N_MOD, d), lambda b, s: (b, 0, 0)),
            _const_spec((1, d)),
            _const_spec(wgu.shape),
            _const_spec(wd.shape),
            _const_spec((1, d)),
        ],
        out_specs=pl.BlockSpec((None, TM_FFN, d), lambda b, s: (b, s, 0)),
        out_shape=jax.ShapeDtypeStruct(x.shape, F32),
        compiler_params=pltpu.CompilerParams(
            dimension_semantics=("arbitrary", "arbitrary"),
            vmem_limit_bytes=VMEM_LIMIT_BYTES),
        name="ffn_final" if final else "ffn",
    )(x, mods, g, wgu, wd, g_final)


def _mix_kernel(sinks_ref, x_ref, mod_ref, g_ref, cos_ref, sin_ref, convw_ref,
                win_ref, wcp_ref, wap_ref, wout_ref, o_ref,
                cu_scr, k_scr, v_scr, q_scr, attn_scr):
    tm = x_ref.shape[0]
    nblk = tm // BLOCK
    s_idx = pl.program_id(1)

    @pl.when(s_idx == 0)
    def _():
        cu_scr[0:SUBLANES, :] = jnp.zeros((SUBLANES, D_MODEL), F32)
        k_scr[:, 0:BLOCK, :] = jnp.zeros((N_KV_HEADS, BLOCK, D_KV), BF16)
        v_scr[:, 0:BLOCK, :] = jnp.zeros((N_KV_HEADS, BLOCK, D_KV), BF16)

    @pl.when(s_idx > 0)
    def _():
        cu_scr[0:SUBLANES, :] = cu_scr[tm:tm + SUBLANES, :]
        for g in range(N_KV_HEADS):
            k_scr[g, 0:BLOCK, :] = k_scr[g, tm:tm + BLOCK, :]
            v_scr[g, 0:BLOCK, :] = v_scr[g, tm:tm + BLOCK, :]

    x = x_ref[...]
    shift = mod_ref[3:4, :]
    scale = mod_ref[4:5, :]
    gate = mod_ref[5:6, :]
    h = _rms_mod(x, g_ref[...], shift, scale).astype(BF16)

    def proj(off, width):
        return jnp.dot(h, win_ref[:, off:off + width], preferred_element_type=F32)

    cu = proj(OFF_C, D_MODEL) * proj(OFF_U, D_MODEL)
    cu_scr[SUBLANES:SUBLANES + tm, :] = cu
    conv = (cu_scr[SUBLANES - 2:SUBLANES - 2 + tm, :] * convw_ref[0:1, :]
            + cu_scr[SUBLANES - 1:SUBLANES - 1 + tm, :] * convw_ref[1:2, :]
            + cu * convw_ref[2:3, :])
    bconv = (proj(OFF_B, D_MODEL) * conv).astype(BF16)
    y_conv = jnp.dot(bconv, wcp_ref[...], preferred_element_type=F32)

    cos = cos_ref[...]
    sin = sin_ref[...]
    lane = lax.broadcasted_iota(jnp.int32, (1, D_KV), 1)
    head_of_lane = (lane % LANES) // HALF
    vhead_of_lane = lane // HEAD_DIM

    k = proj(OFF_K, D_KV)
    k1, k2 = k[:, :LANES], k[:, LANES:]
    kr = jnp.concatenate([k1 * cos - k2 * sin, k2 * cos + k1 * sin], axis=-1)
    v = proj(OFF_V, D_KV)
    for g in range(N_KV_HEADS):
        k_scr[g, BLOCK:BLOCK + tm, :] = jnp.where(head_of_lane == g, kr, 0.0).astype(BF16)
        v_scr[g, BLOCK:BLOCK + tm, :] = jnp.where(vhead_of_lane == g, v, 0.0).astype(BF16)

    q = proj(OFF_Q, D_ATTN)
    qscale = HEAD_DIM ** -0.5
    for j in range(GROUP):
        q1 = q[:, j * D_KV:j * D_KV + LANES]
        q2 = q[:, j * D_KV + LANES:(j + 1) * D_KV]
        qr = jnp.concatenate([(q1 * cos - q2 * sin) * qscale,
                              (q2 * cos + q1 * sin) * qscale], axis=-1).astype(BF16)
        for i in range(nblk):
            q_scr[i, j * BLOCK:(j + 1) * BLOCK, :] = qr[i * BLOCK:(i + 1) * BLOCK, :]

    qi = lax.broadcasted_iota(jnp.int32, (BLOCK, 2 * BLOCK), 0)
    kj = lax.broadcasted_iota(jnp.int32, (BLOCK, 2 * BLOCK), 1)
    band = (kj > qi) & (kj <= qi + BLOCK)
    first_key = jnp.where(s_idx == 0, BLOCK, 0)

    for i in range(nblk):
        valid = band & (kj >= first_key) if i == 0 else band
        qa = q_scr[i]
        p_groups = []
        for g in range(N_KV_HEADS):
            kg = k_scr[g, i * BLOCK:(i + 2) * BLOCK, :]
            s_all = lax.dot_general(qa, kg, (((1,), (1,)), ((), ())),
                                    preferred_element_type=F32)
            p_rows = []
            for j in range(GROUP):
                s = jnp.where(valid, s_all[j * BLOCK:(j + 1) * BLOCK, :], NEG_INF)
                sink = sinks_ref[g * GROUP + j]
                m = jnp.maximum(jnp.max(s, axis=-1, keepdims=True), sink)
                p = jnp.exp(s - m)
                den = jnp.sum(p, axis=-1, keepdims=True) + jnp.exp(sink - m)
                p_rows.append((p * (1.0 / den)).astype(BF16))
            p_groups.append(jnp.concatenate(p_rows, axis=0))
        p_all = jnp.concatenate(p_groups, axis=-1)
        v_all = jnp.concatenate(
            [v_scr[g, i * BLOCK:(i + 2) * BLOCK, :] for g in range(N_KV_HEADS)], axis=0)
        o_all = jnp.dot(p_all, v_all, preferred_element_type=F32)
        for j in range(GROUP):
            attn_scr[i * BLOCK:(i + 1) * BLOCK, j * D_KV:(j + 1) * D_KV] = (
                o_all[j * BLOCK:(j + 1) * BLOCK, :].astype(BF16))

    y_attn = jnp.dot(attn_scr[...], wap_ref[...], preferred_element_type=F32)

    merged = (jax.nn.sigmoid(proj(OFF_ZC, D_MODEL)) * y_conv
              + jax.nn.sigmoid(proj(OFF_ZA, D_MODEL)) * y_attn).astype(BF16)
    o_ref[...] = x + gate * jnp.dot(merged, wout_ref[...], preferred_element_type=F32)


def _mix(x, mods, g, cos_t, sin_t, conv_w, w_in, wcp, wap, wout, sinks):
    bsz, seq, d = x.shape
    tm = TM_MIX
    nblk = tm // BLOCK
    return pl.pallas_call(
        _mix_kernel,
        grid=(bsz, seq // tm),
        in_specs=[
            pl.BlockSpec(memory_space=pltpu.SMEM),
            pl.BlockSpec((None, tm, d), lambda b, s: (b, s, 0)),
            pl.BlockSpec((None, N_MOD, d), lambda b, s: (b, 0, 0)),
            _const_spec((1, d)),
            pl.BlockSpec((tm, LANES), lambda b, s: (s, 0)),
            pl.BlockSpec((tm, LANES), lambda b, s: (s, 0)),
            _const_spec(conv_w.shape),
            _const_spec(w_in.shape),
            _const_spec(wcp.shape),
            _const_spec(wap.shape),
            _const_spec(wout.shape),
        ],
        out_specs=pl.BlockSpec((None, tm, d), lambda b, s: (b, s, 0)),
        scratch_shapes=[
            pltpu.VMEM((SUBLANES + tm, d), F32),
            pltpu.VMEM((N_KV_HEADS, BLOCK + tm, D_KV), BF16),
            pltpu.VMEM((N_KV_HEADS, BLOCK + tm, D_KV), BF16),
            pltpu.VMEM((nblk, GROUP * BLOCK, D_KV), BF16),
            pltpu.VMEM((tm, D_ATTN), BF16),
        ],
        out_shape=jax.ShapeDtypeStruct(x.shape, F32),
        compiler_params=pltpu.CompilerParams(
            dimension_semantics=("arbitrary", "arbitrary"),
            vmem_limit_bytes=VMEM_LIMIT_BYTES),
        name="mixer",
    )(sinks, x, mods, g, cos_t, sin_t, conv_w, w_in, wcp, wap, wout)


def _q_perm():
    idx = np.empty((D_ATTN,), np.int32)
    for j in range(GROUP):
        for half in range(2):
            for g in range(N_KV_HEADS):
                for f in range(HALF):
                    idx[j * D_KV + half * LANES + g * HALF + f] = (
                        (g * GROUP + j) * HEAD_DIM + half * HALF + f)
    return idx


def _k_perm():
    idx = np.empty((D_KV,), np.int32)
    for half in range(2):
        for g in range(N_KV_HEADS):
            for f in range(HALF):
                idx[half * LANES + g * HALF + f] = g * HEAD_DIM + half * HALF + f
    return idx


def _attn_out_perm():
    idx = np.empty((D_ATTN,), np.int32)
    for j in range(GROUP):
        for g in range(N_KV_HEADS):
            for dd in range(HEAD_DIM):
                idx[j * D_KV + g * HEAD_DIM + dd] = (g * GROUP + j) * HEAD_DIM + dd
    return idx


def kernel(x, c, w_ada, b_ada, g_ffn1, w1_gu, w1_down, g_mix, w_in, conv_w,
           w_conv_proj, w_attn_proj, sinks, w_out, g_ffn2, w2_gu, w2_down, g_final):
    bsz, seq, d = x.shape
    depth = w_ada.shape[0]

    inv = 1.0 / (ROPE_THETA ** (jnp.arange(0, HEAD_DIM, 2, dtype=F32) / HEAD_DIM))
    ang = jnp.arange(seq, dtype=F32)[:, None] * inv[None, :]
    cos_t = jnp.tile(jnp.cos(ang), (1, LANES // HALF))
    sin_t = jnp.tile(jnp.sin(ang), (1, LANES // HALF))

    in_perm = np.arange(D_IN, dtype=np.int32)
    in_perm[OFF_Q:OFF_Q + D_ATTN] = OFF_Q + _q_perm()
    in_perm[OFF_K:OFF_K + D_KV] = OFF_K + _k_perm()
    attn_perm = _attn_out_perm()

    c_pad = jnp.pad(c, ((0, SUBLANES - bsz % SUBLANES), (0, 0)))
    gf = g_final.reshape(1, d)

    for l in range(depth):
        mods = _ada(c_pad, w_ada[l], b_ada[l].reshape(1, -1))[:bsz].reshape(bsz, N_MOD, d)

        x = _ffn(x, mods, g_ffn1[l].reshape(1, d), w1_gu[l].astype(BF16),
                 w1_down[l].astype(BF16), gf, mod_base=0, final=False)

        x = _mix(x, mods, g_mix[l].reshape(1, d), cos_t, sin_t, conv_w[l],
                 w_in[l][:, in_perm].astype(BF16), w_conv_proj[l].astype(BF16),
                 w_attn_proj[l][attn_perm, :].astype(BF16), w_out[l].astype(BF16),
                 sinks[l])

        x = _ffn(x, mods, g_ffn2[l].reshape(1, d), w2_gu[l].astype(BF16),
                 w2_down[l].astype(BF16), gf, mod_base=6, final=(l == depth - 1))
    return x
```

```python
import functools

import numpy as np
import jax
import jax.numpy as jnp
from jax import lax
from jax.experimental import pallas as pl
from jax.experimental.pallas import tpu as pltpu

D_MODEL = 1024
CONV_K = 3
HEAD_DIM = 64
HALF = HEAD_DIM // 2
N_HEADS = 16
N_KV_HEADS = 4
GROUP = N_HEADS // N_KV_HEADS
D_ATTN = N_HEADS * HEAD_DIM
D_KV = N_KV_HEADS * HEAD_DIM
WINDOW = 128
BLOCK = 128
ROPE_THETA = 10000.0
D_FF = 2816
N_MOD = 9
EPS = 1e-6
NEG_INF = -1e30

LANES = 128
SUBLANES = 8
VMEM_LIMIT_BYTES = 56 * 1024 * 1024

TM_FFN = 512
TM_MIX = 512
TN_ADA = 1024

OFF_B, OFF_C, OFF_U = 0, D_MODEL, 2 * D_MODEL
OFF_Q = 3 * D_MODEL
OFF_K = OFF_Q + D_ATTN
OFF_V = OFF_K + D_KV
OFF_ZC = OFF_V + D_KV
OFF_ZA = OFF_ZC + D_MODEL
D_IN = OFF_ZA + D_MODEL

BF16 = jnp.bfloat16
F32 = jnp.float32


def _const_spec(shape):
    nd = len(shape)
    return pl.BlockSpec(shape, lambda *_: (0,) * nd, pipeline_mode=pl.Buffered(1))


def _rms_mod(x, g, shift, scale):
    ms = jnp.mean(x * x, axis=-1, keepdims=True)
    y = x * lax.rsqrt(ms + EPS) * g
    return y * (1.0 + scale) + shift


def _ada_kernel(c_ref, w_ref, b_ref, o_ref):
    c = c_ref[...]
    ca = (c * jax.nn.sigmoid(c)).astype(BF16)
    o_ref[...] = jnp.dot(ca, w_ref[...].astype(BF16),
                         preferred_element_type=F32) + b_ref[...]


def _ada(c_pad, w_ada, b_ada):
    rows = c_pad.shape[0]
    n = w_ada.shape[1]
    return pl.pallas_call(
        _ada_kernel,
        grid=(n // TN_ADA,),
        in_specs=[
            pl.BlockSpec((rows, D_MODEL), lambda j: (0, 0)),
            pl.BlockSpec((D_MODEL, TN_ADA), lambda j: (0, j)),
            pl.BlockSpec((1, TN_ADA), lambda j: (0, j)),
        ],
        out_specs=pl.BlockSpec((rows, TN_ADA), lambda j: (0, j)),
        out_shape=jax.ShapeDtypeStruct((rows, n), F32),
        compiler_params=pltpu.CompilerParams(
            dimension_semantics=("arbitrary",),
            vmem_limit_bytes=VMEM_LIMIT_BYTES),
        name="ada_mod",
    )(c_pad, w_ada, b_ada)


def _ffn_kernel(x_ref, mod_ref, g_ref, wgu_ref, wd_ref, gf_ref, o_ref, *, mod_base, final):
    x = x_ref[...]
    shift = mod_ref[mod_base:mod_base + 1, :]
    scale = mod_ref[mod_base + 1:mod_base + 2, :]
    gate = mod_ref[mod_base + 2:mod_base + 3, :]
    h = _rms_mod(x, g_ref[...], shift, scale).astype(BF16)
    gu = jnp.dot(h, wgu_ref[...], preferred_element_type=F32)
    a = gu[:, :D_FF]
    b = gu[:, D_FF:]
    act = (a * jax.nn.sigmoid(a) * b).astype(BF16)
    f = jnp.dot(act, wd_ref[...], preferred_element_type=F32)
    xn = x + (0.5 * gate) * f
    if final:
        ms = jnp.mean(xn * xn, axis=-1, keepdims=True)
        xn = xn * lax.rsqrt(ms + EPS) * gf_ref[...]
    o_ref[...] = xn


def _ffn(x, mods, g, wgu, wd, g_final, *, mod_base, final):
    bsz, seq, d = x.shape
    kern = functools.partial(_ffn_kernel, mod_base=mod_base, final=final)
    return pl.pallas_call(
        kern,
        grid=(bsz, seq // TM_FFN),
        in_specs=[
            pl.BlockSpec((None, TM_FFN, d), lambda b, s: (b, s, 0)),
            pl.BlockSpec((None, N_MOD, d), lambda b, s: (b, 0, 0)),
            _const_spec((1, d)),
            _const_spec(wgu.shape),
            _const_spec(wd.shape),
            _const_spec((1, d)),
        ],
        out_specs=pl.BlockSpec((None, TM_FFN, d), lambda b, s: (b, s, 0)),
        out_shape=jax.ShapeDtypeStruct(x.shape, F32),
        compiler_params=pltpu.CompilerParams(
            dimension_semantics=("arbitrary", "arbitrary"),
            vmem_limit_bytes=VMEM_LIMIT_BYTES),
        name="ffn_final" if final else "ffn",
    )(x, mods, g, wgu, wd, g_final)


def _mix_kernel(sinks_ref, x_ref, mod_ref, g_ref, cos_ref, sin_ref, convw_ref,
                win_ref, wcp_ref, wap_ref, wout_ref, o_ref,
                cu_scr, k_scr, v_scr, q_scr, attn_scr):
    tm = x_ref.shape[0]
    nblk = tm // BLOCK
    s_idx = pl.program_id(1)

    @pl.when(s_idx == 0)
    def _():
        cu_scr[0:SUBLANES, :] = jnp.zeros((SUBLANES, D_MODEL), F32)
        k_scr[:, 0:BLOCK, :] = jnp.zeros((N_KV_HEADS, BLOCK, D_KV), BF16)
        v_scr[:, 0:BLOCK, :] = jnp.zeros((N_KV_HEADS, BLOCK, D_KV), BF16)

    @pl.when(s_idx > 0)
    def _():
        cu_scr[0:SUBLANES, :] = cu_scr[tm:tm + SUBLANES, :]
        for g in range(N_KV_HEADS):
            k_scr[g, 0:BLOCK, :] = k_scr[g, tm:tm + BLOCK, :]
            v_scr[g, 0:BLOCK, :] = v_scr[g, tm:tm + BLOCK, :]

    x = x_ref[...]
    shift = mod_ref[3:4, :]
    scale = mod_ref[4:5, :]
    gate = mod_ref[5:6, :]
    h = _rms_mod(x, g_ref[...], shift, scale).astype(BF16)

    def proj(off, width):
        return jnp.dot(h, win_ref[:, off:off + width], preferred_element_type=F32)

    cu = proj(OFF_C, D_MODEL) * proj(OFF_U, D_MODEL)
    cu_scr[SUBLANES:SUBLANES + tm, :] = cu
    conv = (cu_scr[SUBLANES - 2:SUBLANES - 2 + tm, :] * convw_ref[0:1, :]
            + cu_scr[SUBLANES - 1:SUBLANES - 1 + tm, :] * convw_ref[1:2, :]
            + cu * convw_ref[2:3, :])
    bconv = (proj(OFF_B, D_MODEL) * conv).astype(BF16)
    y_conv = jnp.dot(bconv, wcp_ref[...], preferred_element_type=F32)

    cos = cos_ref[...]
    sin = sin_ref[...]
    lane = lax.broadcasted_iota(jnp.int32, (1, D_KV), 1)
    head_of_lane = (lane % LANES) // HALF
    vhead_of_lane = lane // HEAD_DIM

    k = proj(OFF_K, D_KV)
    k1, k2 = k[:, :LANES], k[:, LANES:]
    kr = jnp.concatenate([k1 * cos - k2 * sin, k2 * cos + k1 * sin], axis=-1)
    v = proj(OFF_V, D_KV)
    for g in range(N_KV_HEADS):
        k_scr[g, BLOCK:BLOCK + tm, :] = jnp.where(head_of_lane == g, kr, 0.0).astype(BF16)
        v_scr[g, BLOCK:BLOCK + tm, :] = jnp.where(vhead_of_lane == g, v, 0.0).astype(BF16)

    q = proj(OFF_Q, D_ATTN)
    qscale = HEAD_DIM ** -0.5
    for j in range(GROUP):
        q1 = q[:, j * D_KV:j * D_KV + LANES]
        q2 = q[:, j * D_KV + LANES:(j + 1) * D_KV]
        qr = jnp.concatenate([(q1 * cos - q2 * sin) * qscale,
                              (q2 * cos + q1 * sin) * qscale], axis=-1).astype(BF16)
        for i in range(nblk):
            q_scr[i, j * BLOCK:(j + 1) * BLOCK, :] = qr[i * BLOCK:(i + 1) * BLOCK, :]

    qi = lax.broadcasted_iota(jnp.int32, (BLOCK, 2 * BLOCK), 0)
    kj = lax.broadcasted_iota(jnp.int32, (BLOCK, 2 * BLOCK), 1)
    band = (kj > qi) & (kj <= qi + BLOCK)
    first_key = jnp.where(s_idx == 0, BLOCK, 0)

    for i in range(nblk):
        valid = band & (kj >= first_key) if i == 0 else band
        qa = q_scr[i]
        p_groups = []
        for g in range(N_KV_HEADS):
            kg = k_scr[g, i * BLOCK:(i + 2) * BLOCK, :]
            s_all = lax.dot_general(qa, kg, (((1,), (1,)), ((), ())),
                                    preferred_element_type=F32)
            p_rows = []
            for j in range(GROUP):
                s = jnp.where(valid, s_all[j * BLOCK:(j + 1) * BLOCK, :], NEG_INF)
                sink = sinks_ref[g * GROUP + j]
                m = jnp.maximum(jnp.max(s, axis=-1, keepdims=True), sink)
                p = jnp.exp(s - m)
                den = jnp.sum(p, axis=-1, keepdims=True) + jnp.exp(sink - m)
                p_rows.append((p * (1.0 / den)).astype(BF16))
            p_groups.append(jnp.concatenate(p_rows, axis=0))
        p_all = jnp.concatenate(p_groups, axis=-1)
        v_all = jnp.concatenate(
            [v_scr[g, i * BLOCK:(i + 2) * BLOCK, :] for g in range(N_KV_HEADS)], axis=0)
        o_all = jnp.dot(p_all, v_all, preferred_element_type=F32)
        for j in range(GROUP):
            attn_scr[i * BLOCK:(i + 1) * BLOCK, j * D_KV:(j + 1) * D_KV] = (
                o_all[j * BLOCK:(j + 1) * BLOCK, :].astype(BF16))

    y_attn = jnp.dot(attn_scr[...], wap_ref[...], preferred_element_type=F32)

    merged = (jax.nn.sigmoid(proj(OFF_ZC, D_MODEL)) * y_conv
              + jax.nn.sigmoid(proj(OFF_ZA, D_MODEL)) * y_attn).astype(BF16)
    o_ref[...] = x + gate * jnp.dot(merged, wout_ref[...], preferred_element_type=F32)


def _mix(x, mods, g, cos_t, sin_t, conv_w, w_in, wcp, wap, wout, sinks):
    bsz, seq, d = x.shape
    tm = TM_MIX
    nblk = tm // BLOCK
    return pl.pallas_call(
        _mix_kernel,
        grid=(bsz, seq // tm),
        in_specs=[
            pl.BlockSpec(memory_space=pltpu.SMEM),
            pl.BlockSpec((None, tm, d), lambda b, s: (b, s, 0)),
            pl.BlockSpec((None, N_MOD, d), lambda b, s: (b, 0, 0)),
            _const_spec((1, d)),
            pl.BlockSpec((tm, LANES), lambda b, s: (s, 0)),
            pl.BlockSpec((tm, LANES), lambda b, s: (s, 0)),
            _const_spec(conv_w.shape),
            _const_spec(w_in.shape),
            _const_spec(wcp.shape),
            _const_spec(wap.shape),
            _const_spec(wout.shape),
        ],
        out_specs=pl.BlockSpec((None, tm, d), lambda b, s: (b, s, 0)),
        scratch_shapes=[
            pltpu.VMEM((SUBLANES + tm, d), F32),
            pltpu.VMEM((N_KV_HEADS, BLOCK + tm, D_KV), BF16),
            pltpu.VMEM((N_KV_HEADS, BLOCK + tm, D_KV), BF16),
            pltpu.VMEM((nblk, GROUP * BLOCK, D_KV), BF16),
            pltpu.VMEM((tm, D_ATTN), BF16),
        ],
        out_shape=jax.ShapeDtypeStruct(x.shape, F32),
        compiler_params=pltpu.CompilerParams(
            dimension_semantics=("arbitrary", "arbitrary"),
            vmem_limit_bytes=VMEM_LIMIT_BYTES),
        name="mixer",
    )(sinks, x, mods, g, cos_t, sin_t, conv_w, w_in, wcp, wap, wout)


def _q_perm():
    idx = np.empty((D_ATTN,), np.int32)
    for j in range(GROUP):
        for half in range(2):
            for g in range(N_KV_HEADS):
                for f in range(HALF):
                    idx[j * D_KV + half * LANES + g * HALF + f] = (
                        (g * GROUP + j) * HEAD_DIM + half * HALF + f)
    return idx


def _k_perm():
    idx = np.empty((D_KV,), np.int32)
    for half in range(2):
        for g in range(N_KV_HEADS):
            for f in range(HALF):
                idx[half * LANES + g * HALF + f] = g * HEAD_DIM + half * HALF + f
    return idx


def _attn_out_perm():
    idx = np.empty((D_ATTN,), np.int32)
    for j in range(GROUP):
        for g in range(N_KV_HEADS):
            for dd in range(HEAD_DIM):
                idx[j * D_KV + g * HEAD_DIM + dd] = (g * GROUP + j) * HEAD_DIM + dd
    return idx


def kernel(x, c, w_ada, b_ada, g_ffn1, w1_gu, w1_down, g_mix, w_in, conv_w,
           w_conv_proj, w_attn_proj, sinks, w_out, g_ffn2, w2_gu, w2_down, g_final):
    bsz, seq, d = x.shape
    depth = w_ada.shape[0]

    inv = 1.0 / (ROPE_THETA ** (jnp.arange(0, HEAD_DIM, 2, dtype=F32) / HEAD_DIM))
    ang = jnp.arange(seq, dtype=F32)[:, None] * inv[None, :]
    cos_t = jnp.tile(jnp.cos(ang), (1, LANES // HALF))
    sin_t = jnp.tile(jnp.sin(ang), (1, LANES // HALF))

    in_perm = np.arange(D_IN, dtype=np.int32)
    in_perm[OFF_Q:OFF_Q + D_ATTN] = OFF_Q + _q_perm()
    in_perm[OFF_K:OFF_K + D_KV] = OFF_K + _k_perm()
    attn_perm = _attn_out_perm()

    c_pad = jnp.pad(c, ((0, SUBLANES - bsz % SUBLANES), (0, 0)))
    gf = g_final.reshape(1, d)

    for l in range(depth):
        mods = _ada(c_pad, w_ada[l], b_ada[l].reshape(1, -1))[:bsz].reshape(bsz, N_MOD, d)

        x = _ffn(x, mods, g_ffn1[l].reshape(1, d), w1_gu[l].astype(BF16),
                 w1_down[l].astype(BF16), gf, mod_base=0, final=False)

        x = _mix(x, mods, g_mix[l].reshape(1, d), cos_t, sin_t, conv_w[l],
                 w_in[l][:, in_perm].astype(BF16), w_conv_proj[l].astype(BF16),
                 w_attn_proj[l][attn_perm, :].astype(BF16), w_out[l].astype(BF16),
                 sinks[l])

        x = _ffn(x, mods, g_ffn2[l].reshape(1, d), w2_gu[l].astype(BF16),
                 w2_down[l].astype(BF16), gf, mod_base=6, final=(l == depth - 1))
    return x
```

```python
import functools

import numpy as np
import jax
import jax.numpy as jnp
from jax import lax
from jax.experimental import pallas as pl
from jax.experimental.pallas import tpu as pltpu

D_MODEL = 1024
CONV_K = 3
HEAD_DIM = 64
HALF = HEAD_DIM // 2
N_HEADS = 16
N_KV_HEADS = 4
GROUP = N_HEADS // N_KV_HEADS
D_ATTN = N_HEADS * HEAD_DIM
D_KV = N_KV_HEADS * HEAD_DIM
WINDOW = 128
BLOCK = 128
ROPE_THETA = 10000.0
D_FF = 2816
N_MOD = 9
EPS = 1e-6
NEG_INF = -1e30

LANES = 128
SUBLANES = 8
VMEM_LIMIT_BYTES = 56 * 1024 * 1024

TM_FFN = 512
TM_MIX = 512
TN_ADA = 1024

OFF_B, OFF_C, OFF_U = 0, D_MODEL, 2 * D_MODEL
OFF_Q = 3 * D_MODEL
OFF_K = OFF_Q + D_ATTN
OFF_V = OFF_K + D_KV
OFF_ZC = OFF_V + D_KV
OFF_ZA = OFF_ZC + D_MODEL
D_IN = OFF_ZA + D_MODEL

BF16 = jnp.bfloat16
F32 = jnp.float32


def _const_spec(shape):
    nd = len(shape)
    return pl.BlockSpec(shape, lambda *_: (0,) * nd, pipeline_mode=pl.Buffered(1))


def _rms_mod(x, g, shift, scale):
    ms = jnp.mean(x * x, axis=-1, keepdims=True)
    y = x * lax.rsqrt(ms + EPS) * g
    return y * (1.0 + scale) + shift


def _ada_kernel(c_ref, w_ref, b_ref, o_ref):
    c = c_ref[...]
    ca = (c * jax.nn.sigmoid(c)).astype(BF16)
    o_ref[...] = jnp.dot(ca, w_ref[...].astype(BF16),
                         preferred_element_type=F32) + b_ref[...]


def _ada(c_pad, w_ada, b_ada):
    rows = c_pad.shape[0]
    n = w_ada.shape[1]
    return pl.pallas_call(
        _ada_kernel,
        grid=(n // TN_ADA,),
        in_specs=[
            pl.BlockSpec((rows, D_MODEL), lambda j: (0, 0)),
            pl.BlockSpec((D_MODEL, TN_ADA), lambda j: (0, j)),
            pl.BlockSpec((1, TN_ADA), lambda j: (0, j)),
        ],
        out_specs=pl.BlockSpec((rows, TN_ADA), lambda j: (0, j)),
        out_shape=jax.ShapeDtypeStruct((rows, n), F32),
        compiler_params=pltpu.CompilerParams(
            dimension_semantics=("arbitrary",),
            vmem_limit_bytes=VMEM_LIMIT_BYTES),
        name="ada_mod",
    )(c_pad, w_ada, b_ada)


def _ffn_kernel(x_ref, mod_ref, g_ref, wgu_ref, wd_ref, gf_ref, o_ref, *, mod_base, final):
    x = x_ref[...]
    shift = mod_ref[mod_base:mod_base + 1, :]
    scale = mod_ref[mod_base + 1:mod_base + 2, :]
    gate = mod_ref[mod_base + 2:mod_base + 3, :]
    h = _rms_mod(x, g_ref[...], shift, scale).astype(BF16)
    gu = jnp.dot(h, wgu_ref[...], preferred_element_type=F32)
    a = gu[:, :D_FF]
    b = gu[:, D_FF:]
    act = (a * jax.nn.sigmoid(a) * b).astype(BF16)
    f = jnp.dot(act, wd_ref[...], preferred_element_type=F32)
    xn = x + (0.5 * gate) * f
    if final:
        ms = jnp.mean(xn * xn, axis=-1, keepdims=True)
        xn = xn * lax.rsqrt(ms + EPS) * gf_ref[...]
    o_ref[...] = xn


def _ffn(x, mods, g, wgu, wd, g_final, *, mod_base, final):
    bsz, seq, d = x.shape
    kern = functools.partial(_ffn_kernel, mod_base=mod_base, final=final)
    return pl.pallas_call(
        kern,
        grid=(bsz, seq // TM_FFN),
        in_specs=[
            pl.BlockSpec((None, TM_FFN, d), lambda b, s: (b, s, 0)),
            pl.BlockSpec((None, N_MOD, d), lambda b, s: (b, 0, 0)),
            _const_spec((1, d)),
            _const_spec(wgu.shape),
            _const_spec(wd.shape),
            _const_spec((1, d)),
        ],
        out_specs=pl.BlockSpec((None, TM_FFN, d), lambda b, s: (b, s, 0)),
        out_shape=jax.ShapeDtypeStruct(x.shape, F32),
        compiler_params=pltpu.CompilerParams(
            dimension_semantics=("arbitrary", "arbitrary"),
            vmem_limit_bytes=VMEM_LIMIT_BYTES),
        name="ffn_final" if final else "ffn",
    )(x, mods, g, wgu, wd, g_final)


def _mix_kernel(sinks_ref, x_ref, mod_ref, g_ref, cos_ref, sin_ref, convw_ref,
                win_ref, wcp_ref, wap_ref, wout_ref, o_ref,
                cu_scr, k_scr, vt_scr, q_scr, attn_scr):
    tm = x_ref.shape[0]
    nblk = tm // BLOCK
    s_idx = pl.program_id(1)

    @pl.when(s_idx == 0)
    def _():
        cu_scr[0:SUBLANES, :] = jnp.zeros((SUBLANES, D_MODEL), F32)
        k_scr[:, 0:BLOCK, :] = jnp.zeros((N_KV_HEADS, BLOCK, D_KV), BF16)
        vt_scr[:, :, 0:BLOCK] = jnp.zeros((N_KV_HEADS, D_KV, BLOCK), BF16)

    @pl.when(s_idx > 0)
    def _():
        cu_scr[0:SUBLANES, :] = cu_scr[tm:tm + SUBLANES, :]
        for g in range(N_KV_HEADS):
            k_scr[g, 0:BLOCK, :] = k_scr[g, tm:tm + BLOCK, :]
            vt_scr[g, :, 0:BLOCK] = vt_scr[g, :, tm:tm + BLOCK]

    x = x_ref[...]
    shift = mod_ref[3:4, :]
    scale = mod_ref[4:5, :]
    gate = mod_ref[5:6, :]
    h = _rms_mod(x, g_ref[...], shift, scale).astype(BF16)

    def proj(off, width):
        return jnp.dot(h, win_ref[:, off:off + width], preferred_element_type=F32)

    cu = proj(OFF_C, D_MODEL) * proj(OFF_U, D_MODEL)
    cu_scr[SUBLANES:SUBLANES + tm, :] = cu
    conv = (cu_scr[SUBLANES - 2:SUBLANES - 2 + tm, :] * convw_ref[0:1, :]
            + cu_scr[SUBLANES - 1:SUBLANES - 1 + tm, :] * convw_ref[1:2, :]
            + cu * convw_ref[2:3, :])
    bconv = (proj(OFF_B, D_MODEL) * conv).astype(BF16)
    y_conv = jnp.dot(bconv, wcp_ref[...], preferred_element_type=F32)

    cos = cos_ref[...]
    sin = sin_ref[...]
    lane = lax.broadcasted_iota(jnp.int32, (1, D_KV), 1)
    head_of_lane = (lane % LANES) // HALF
    vrow = lax.broadcasted_iota(jnp.int32, (D_KV, 1), 0)
    head_of_vrow = vrow // HEAD_DIM

    k = proj(OFF_K, D_KV)
    k1, k2 = k[:, :LANES], k[:, LANES:]
    kr = jnp.concatenate([k1 * cos - k2 * sin, k2 * cos + k1 * sin], axis=-1)
    vt = proj(OFF_V, D_KV).T
    for g in range(N_KV_HEADS):
        k_scr[g, BLOCK:BLOCK + tm, :] = jnp.where(head_of_lane == g, kr, 0.0).astype(BF16)
        vt_scr[g, :, BLOCK:BLOCK + tm] = jnp.where(head_of_vrow == g, vt, 0.0).astype(BF16)

    q = proj(OFF_Q, D_ATTN)
    qscale = HEAD_DIM ** -0.5
    for j in range(GROUP):
        q1 = q[:, j * D_KV:j * D_KV + LANES]
        q2 = q[:, j * D_KV + LANES:(j + 1) * D_KV]
        qr = jnp.concatenate([(q1 * cos - q2 * sin) * qscale,
                              (q2 * cos + q1 * sin) * qscale], axis=-1).astype(BF16)
        for i in range(nblk):
            q_scr[i, j * BLOCK:(j + 1) * BLOCK, :] = qr[i * BLOCK:(i + 1) * BLOCK, :]

    key_slot = lax.broadcasted_iota(jnp.int32, (BLOCK, GROUP * BLOCK), 0)
    query = lax.broadcasted_iota(jnp.int32, (BLOCK, GROUP * BLOCK), 1)
    from_prev = key_slot > (query % BLOCK)
    neg_first = jnp.where(s_idx == 0, NEG_INF, 0.0)
    sink_rows = [
        jnp.concatenate([jnp.full((1, BLOCK), sinks_ref[g * GROUP + j], F32)
                         for j in range(GROUP)], axis=1)
        for g in range(N_KV_HEADS)]

    for i in range(nblk):
        k_all = jnp.concatenate(
            [k_scr[g, i * BLOCK:(i + 2) * BLOCK, :] for g in range(N_KV_HEADS)], axis=0)
        st_all = lax.dot_general(k_all, q_scr[i], (((1,), (1,)), ((), ())),
                                 preferred_element_type=F32)
        pt_parts = []
        for g in range(N_KV_HEADS):
            s_prev = st_all[2 * g * BLOCK:(2 * g + 1) * BLOCK, :]
            s_cur = st_all[(2 * g + 1) * BLOCK:(2 * g + 2) * BLOCK, :]
            if i == 0:
                s_prev = s_prev + neg_first
            s = jnp.where(from_prev, s_prev, s_cur)
            sink = sink_rows[g]
            m = jnp.maximum(jnp.max(s, axis=0, keepdims=True), sink)
            p = jnp.exp(s - m)
            den = jnp.sum(p, axis=0, keepdims=True) + jnp.exp(sink - m)
            pn = p * (1.0 / den)
            pt_parts.append(jnp.where(from_prev, pn, 0.0).astype(BF16))
            pt_parts.append(jnp.where(from_prev, 0.0, pn).astype(BF16))
        pt_all = jnp.concatenate(pt_parts, axis=0)
        vt_all = jnp.concatenate(
            [vt_scr[g, :, i * BLOCK:(i + 2) * BLOCK] for g in range(N_KV_HEADS)], axis=1)
        o_all = jnp.dot(vt_all, pt_all, preferred_element_type=F32).T
        for j in range(GROUP):
            attn_scr[i * BLOCK:(i + 1) * BLOCK, j * D_KV:(j + 1) * D_KV] = (
                o_all[j * BLOCK:(j + 1) * BLOCK, :].astype(BF16))

    y_attn = jnp.dot(attn_scr[...], wap_ref[...], preferred_element_type=F32)

    merged = (jax.nn.sigmoid(proj(OFF_ZC, D_MODEL)) * y_conv
              + jax.nn.sigmoid(proj(OFF_ZA, D_MODEL)) * y_attn).astype(BF16)
    o_ref[...] = x + gate * jnp.dot(merged, wout_ref[...], preferred_element_type=F32)


def _mix(x, mods, g, cos_t, sin_t, conv_w, w_in, wcp, wap, wout, sinks):
    bsz, seq, d = x.shape
    tm = TM_MIX
    nblk = tm // BLOCK
    return pl.pallas_call(
        _mix_kernel,
        grid=(bsz, seq // tm),
        in_specs=[
            pl.BlockSpec(memory_space=pltpu.SMEM),
            pl.BlockSpec((None, tm, d), lambda b, s: (b, s, 0)),
            pl.BlockSpec((None, N_MOD, d), lambda b, s: (b, 0, 0)),
            _const_spec((1, d)),
            pl.BlockSpec((tm, LANES), lambda b, s: (s, 0)),
            pl.BlockSpec((tm, LANES), lambda b, s: (s, 0)),
            _const_spec(conv_w.shape),
            _const_spec(w_in.shape),
            _const_spec(wcp.shape),
            _const_spec(wap.shape),
            _const_spec(wout.shape),
        ],
        out_specs=pl.BlockSpec((None, tm, d), lambda b, s: (b, s, 0)),
        scratch_shapes=[
            pltpu.VMEM((SUBLANES + tm, d), F32),
            pltpu.VMEM((N_KV_HEADS, BLOCK + tm, D_KV), BF16),
            pltpu.VMEM((N_KV_HEADS, D_KV, BLOCK + tm), BF16),
            pltpu.VMEM((nblk, GROUP * BLOCK, D_KV), BF16),
            pltpu.VMEM((tm, D_ATTN), BF16),
        ],
        out_shape=jax.ShapeDtypeStruct(x.shape, F32),
        compiler_params=pltpu.CompilerParams(
            dimension_semantics=("arbitrary", "arbitrary"),
            vmem_limit_bytes=VMEM_LIMIT_BYTES),
        name="mixer",
    )(sinks, x, mods, g, cos_t, sin_t, conv_w, w_in, wcp, wap, wout)


def _q_perm():
    idx = np.empty((D_ATTN,), np.int32)
    for j in range(GROUP):
        for half in range(2):
            for g in range(N_KV_HEADS):
                for f in range(HALF):
                    idx[j * D_KV + half * LANES + g * HALF + f] = (
                        (g * GROUP + j) * HEAD_DIM + half * HALF + f)
    return idx


def _k_perm():
    idx = np.empty((D_KV,), np.int32)
    for half in range(2):
        for g in range(N_KV_HEADS):
            for f in range(HALF):
                idx[half * LANES + g * HALF + f] = g * HEAD_DIM + half * HALF + f
    return idx


def _attn_out_perm():
    idx = np.empty((D_ATTN,), np.int32)
    for j in range(GROUP):
        for g in range(N_KV_HEADS):
            for dd in range(HEAD_DIM):
                idx[j * D_KV + g * HEAD_DIM + dd] = (g * GROUP + j) * HEAD_DIM + dd
    return idx


def kernel(x, c, w_ada, b_ada, g_ffn1, w1_gu, w1_down, g_mix, w_in, conv_w,
           w_conv_proj, w_attn_proj, sinks, w_out, g_ffn2, w2_gu, w2_down, g_final):
    bsz, seq, d = x.shape
    depth = w_ada.shape[0]

    inv = 1.0 / (ROPE_THETA ** (jnp.arange(0, HEAD_DIM, 2, dtype=F32) / HEAD_DIM))
    ang = jnp.arange(seq, dtype=F32)[:, None] * inv[None, :]
    cos_t = jnp.tile(jnp.cos(ang), (1, LANES // HALF))
    sin_t = jnp.tile(jnp.sin(ang), (1, LANES // HALF))

    in_perm = np.arange(D_IN, dtype=np.int32)
    in_perm[OFF_Q:OFF_Q + D_ATTN] = OFF_Q + _q_perm()
    in_perm[OFF_K:OFF_K + D_KV] = OFF_K + _k_perm()
    attn_perm = _attn_out_perm()

    c_pad = jnp.pad(c, ((0, SUBLANES - bsz % SUBLANES), (0, 0)))
    gf = g_final.reshape(1, d)

    for l in range(depth):
        mods = _ada(c_pad, w_ada[l], b_ada[l].reshape(1, -1))[:bsz].reshape(bsz, N_MOD, d)

        x = _ffn(x, mods, g_ffn1[l].reshape(1, d), w1_gu[l].astype(BF16),
                 w1_down[l].astype(BF16), gf, mod_base=0, final=False)

        x = _mix(x, mods, g_mix[l].reshape(1, d), cos_t, sin_t, conv_w[l],
                 w_in[l][:, in_perm].astype(BF16), w_conv_proj[l].astype(BF16),
                 w_attn_proj[l][attn_perm, :].astype(BF16), w_out[l].astype(BF16),
                 sinks[l])

        x = _ffn(x, mods, g_ffn2[l].reshape(1, d), w2_gu[l].astype(BF16),
                 w2_down[l].astype(BF16), gf, mod_base=6, final=(l == depth - 1))
    return x
```

```python
import functools
from typing import Callable, NamedTuple

import jax
import jax.numpy as jnp
from jax import lax
from jax.experimental import pallas as pl
from jax.experimental.pallas import tpu as pltpu

D_MODEL = 1024
CONV_K = 3
HEAD_DIM = 64
HALF = HEAD_DIM // 2
N_HEADS = 16
N_KV_HEADS = 4
GROUP = N_HEADS // N_KV_HEADS
D_ATTN = N_HEADS * HEAD_DIM
D_KV = N_KV_HEADS * HEAD_DIM
WINDOW = 128
BLOCK = 128
ROPE_THETA = 10000.0
D_FF = 2816
N_MOD = 9
EPS = 1e-6
NEG_INF = -1e30

LANES = 128
SUBLANES = 8
BF16_SUBLANES = 16
VMEM_LIMIT_BYTES = 56 * 1024 * 1024

TM_FFN = 512
TM_MIX = 512
ADA_STEPS = 8

OFF_B, OFF_C, OFF_U = 0, D_MODEL, 2 * D_MODEL
OFF_Q = 3 * D_MODEL
OFF_K = OFF_Q + D_ATTN
OFF_V = OFF_K + D_KV
OFF_ZC = OFF_V + D_KV
OFF_ZA = OFF_ZC + D_MODEL

BF16 = jnp.bfloat16
F32 = jnp.float32


def _const_spec(shape):
    nd = len(shape)
    return pl.BlockSpec(shape, lambda *_: (0,) * nd, pipeline_mode=pl.Buffered(1))


def _layer_spec(shape, layer):
    nd = len(shape)
    return pl.BlockSpec((None,) + tuple(shape), lambda *_: (layer,) + (0,) * nd,
                        pipeline_mode=pl.Buffered(1))


class _Cast(NamedTuple):
    src: jax.Array
    layer: int
    block: tuple
    in_index: Callable
    out_index: Callable


def _grid_cast(src, layer, steps):
    rows, cols = src.shape[1:]
    col_blocks = 1
    while (rows * col_blocks) % (steps * BF16_SUBLANES):
        col_blocks *= 2
    row_blocks = steps // col_blocks
    idx = lambda t: (t // col_blocks, t % col_blocks)
    return _Cast(src, layer, (rows // row_blocks, cols // col_blocks), idx, idx)


def _cast_specs(casts, linear_step):
    in_specs, out_specs, out_shapes = [], [], []
    for cst in casts:
        in_specs.append(pl.BlockSpec(
            (None,) + cst.block,
            lambda *idx, cst=cst: (cst.layer,) + cst.in_index(linear_step(*idx))))
        out_specs.append(pl.BlockSpec(
            cst.block, lambda *idx, cst=cst: cst.out_index(linear_step(*idx))))
        out_shapes.append(jax.ShapeDtypeStruct(cst.src.shape[1:], BF16))
    return in_specs, out_specs, out_shapes


def _run_casts(src_refs, dst_refs):
    for src, dst in zip(src_refs, dst_refs):
        dst[...] = src[...].astype(BF16)


def _rms_mod(x, g, shift, scale):
    ms = jnp.mean(x * x, axis=-1, keepdims=True)
    y = x * lax.rsqrt(ms + EPS) * g
    return y * (1.0 + scale) + shift


def _ada_kernel(*refs, n_cast):
    c_ref, w_ref, b_ref = refs[:3]
    cast_in = refs[3:3 + n_cast]
    o_ref = refs[3 + n_cast]
    cast_out = refs[4 + n_cast:]
    c = c_ref[...]
    ca = (c * jax.nn.sigmoid(c)).astype(BF16)
    o_ref[...] = jnp.dot(ca, w_ref[...].astype(BF16),
                         preferred_element_type=F32) + b_ref[...]
    _run_casts(cast_in, cast_out)


def _ada(c_pad, w_ada, b_ada, layer, casts):
    rows = c_pad.shape[0]
    n = w_ada.shape[2]
    tn = n // ADA_STEPS
    cin, cout, cshape = _cast_specs(casts, lambda j: j)
    outs = pl.pallas_call(
        functools.partial(_ada_kernel, n_cast=len(casts)),
        grid=(ADA_STEPS,),
        in_specs=[
            pl.BlockSpec((rows, D_MODEL), lambda j: (0, 0)),
            pl.BlockSpec((None, D_MODEL, tn), lambda j: (layer, 0, j)),
            pl.BlockSpec((None, 1, tn), lambda j: (layer, 0, j)),
        ] + cin,
        out_specs=[pl.BlockSpec((rows, tn), lambda j: (0, j))] + cout,
        out_shape=[jax.ShapeDtypeStruct((rows, n), F32)] + cshape,
        compiler_params=pltpu.CompilerParams(
            dimension_semantics=("arbitrary",),
            vmem_limit_bytes=VMEM_LIMIT_BYTES),
        name="ada_mod",
    )(c_pad, w_ada, b_ada, *[cst.src for cst in casts])
    return outs[0], outs[1:]


def _ffn_kernel(*refs, mod_base, final, n_cast):
    x_ref, mod_ref, g_ref, wgu_ref, wd_ref, gf_ref = refs[:6]
    cast_in = refs[6:6 + n_cast]
    o_ref = refs[6 + n_cast]
    cast_out = refs[7 + n_cast:]
    x = x_ref[...]
    shift = mod_ref[mod_base:mod_base + 1, :]
    scale = mod_ref[mod_base + 1:mod_base + 2, :]
    gate = mod_ref[mod_base + 2:mod_base + 3, :]
    h = _rms_mod(x, g_ref[...], shift, scale).astype(BF16)
    gu = jnp.dot(h, wgu_ref[...], preferred_element_type=F32)
    a = gu[:, :D_FF]
    b = gu[:, D_FF:]
    act = (a * jax.nn.sigmoid(a) * b).astype(BF16)
    f = jnp.dot(act, wd_ref[...], preferred_element_type=F32)
    xn = x + (0.5 * gate) * f
    if final:
        ms = jnp.mean(xn * xn, axis=-1, keepdims=True)
        xn = xn * lax.rsqrt(ms + EPS) * gf_ref[...]
    o_ref[...] = xn
    _run_casts(cast_in, cast_out)


def _ffn(x, mods, g, layer, wgu, wd, g_final, casts, *, mod_base, final):
    bsz, seq, d = x.shape
    n_s = seq // TM_FFN
    cin, cout, cshape = _cast_specs(casts, lambda b, s: b * n_s + s)
    kern = functools.partial(_ffn_kernel, mod_base=mod_base, final=final, n_cast=len(casts))
    outs = pl.pallas_call(
        kern,
        grid=(bsz, n_s),
        in_specs=[
            pl.BlockSpec((None, TM_FFN, d), lambda b, s: (b, s, 0)),
            pl.BlockSpec((None, N_MOD, d), lambda b, s: (b, 0, 0)),
            _layer_spec((1, d), layer),
            _const_spec(wgu.shape),
            _const_spec(wd.shape),
            _const_spec((1, d)),
        ] + cin,
        out_specs=[pl.BlockSpec((None, TM_FFN, d), lambda b, s: (b, s, 0))] + cout,
        out_shape=[jax.ShapeDtypeStruct(x.shape, F32)] + cshape,
        compiler_params=pltpu.CompilerParams(
            dimension_semantics=("arbitrary", "arbitrary"),
            vmem_limit_bytes=VMEM_LIMIT_BYTES),
        name="ffn_final" if final else "ffn",
    )(x, mods, g, wgu, wd, g_final, *[cst.src for cst in casts])
    return outs[0], outs[1:]


def _mix_kernel(*refs, layer, n_cast):
    (sinks_ref, x_ref, mod_ref, g_ref, cos_ref, sin_ref, convw_ref,
     win_ref, wqk_ref, wcp_ref, wap_ref, wout_ref) = refs[:12]
    cast_in = refs[12:12 + n_cast]
    o_ref = refs[12 + n_cast]
    cast_out = refs[13 + n_cast:13 + 2 * n_cast]
    cu_scr, k_scr, vt_scr, q_scr, attn_scr = refs[13 + 2 * n_cast:]
    tm = x_ref.shape[0]
    nblk = tm // BLOCK
    s_idx = pl.program_id(1)

    @pl.when(s_idx == 0)
    def _():
        cu_scr[0:SUBLANES, :] = jnp.zeros((SUBLANES, D_MODEL), F32)
        k_scr[:, 0:BLOCK, :] = jnp.zeros((N_KV_HEADS, BLOCK, D_KV), BF16)
        vt_scr[:, :, 0:BLOCK] = jnp.zeros((N_KV_HEADS, D_KV, BLOCK), BF16)

    @pl.when(s_idx > 0)
    def _():
        cu_scr[0:SUBLANES, :] = cu_scr[tm:tm + SUBLANES, :]
        for g in range(N_KV_HEADS):
            k_scr[g, 0:BLOCK, :] = k_scr[g, tm:tm + BLOCK, :]
            vt_scr[g, :, 0:BLOCK] = vt_scr[g, :, tm:tm + BLOCK]

    x = x_ref[...]
    shift = mod_ref[3:4, :]
    scale = mod_ref[4:5, :]
    gate = mod_ref[5:6, :]
    h = _rms_mod(x, g_ref[...], shift, scale).astype(BF16)

    def proj(off, width):
        return jnp.dot(h, win_ref[:, off:off + width], preferred_element_type=F32)

    cu = proj(OFF_C, D_MODEL) * proj(OFF_U, D_MODEL)
    cu_scr[SUBLANES:SUBLANES + tm, :] = cu
    conv = (cu_scr[SUBLANES - 2:SUBLANES - 2 + tm, :] * convw_ref[0:1, :]
            + cu_scr[SUBLANES - 1:SUBLANES - 1 + tm, :] * convw_ref[1:2, :]
            + cu * convw_ref[2:3, :])
    bconv = (proj(OFF_B, D_MODEL) * conv).astype(BF16)
    y_conv = jnp.dot(bconv, wcp_ref[...], preferred_element_type=F32)

    cos = cos_ref[...]
    sin = sin_ref[...]
    lane = lax.broadcasted_iota(jnp.int32, (1, D_KV), 1)
    head_of_lane = (lane % LANES) // HALF
    vrow = lax.broadcasted_iota(jnp.int32, (D_KV, 1), 0)
    head_of_vrow = vrow // HEAD_DIM

    k = jnp.dot(h, wqk_ref[:, D_ATTN:], preferred_element_type=F32)
    k1, k2 = k[:, :LANES], k[:, LANES:]
    kr = jnp.concatenate([k1 * cos - k2 * sin, k2 * cos + k1 * sin], axis=-1)
    vt = proj(OFF_V, D_KV).T
    for g in range(N_KV_HEADS):
        k_scr[g, BLOCK:BLOCK + tm, :] = jnp.where(head_of_lane == g, kr, 0.0).astype(BF16)
        vt_scr[g, :, BLOCK:BLOCK + tm] = jnp.where(head_of_vrow == g, vt, 0.0).astype(BF16)

    q = jnp.dot(h, wqk_ref[:, :D_ATTN], preferred_element_type=F32)
    qscale = HEAD_DIM ** -0.5
    for j in range(GROUP):
        q1 = q[:, j * D_KV:j * D_KV + LANES]
        q2 = q[:, j * D_KV + LANES:(j + 1) * D_KV]
        qr = jnp.concatenate([(q1 * cos - q2 * sin) * qscale,
                              (q2 * cos + q1 * sin) * qscale], axis=-1).astype(BF16)
        for i in range(nblk):
            q_scr[i, j * BLOCK:(j + 1) * BLOCK, :] = qr[i * BLOCK:(i + 1) * BLOCK, :]

    key_slot = lax.broadcasted_iota(jnp.int32, (BLOCK, GROUP * BLOCK), 0)
    query = lax.broadcasted_iota(jnp.int32, (BLOCK, GROUP * BLOCK), 1)
    from_prev = key_slot > (query % BLOCK)
    neg_first = jnp.where(s_idx == 0, NEG_INF, 0.0)
    sink_rows = [
        jnp.concatenate([jnp.full((1, BLOCK), sinks_ref[layer, g * GROUP + j], F32)
                         for j in range(GROUP)], axis=1)
        for g in range(N_KV_HEADS)]

    for i in range(nblk):
        k_all = jnp.concatenate(
            [k_scr[g, i * BLOCK:(i + 2) * BLOCK, :] for g in range(N_KV_HEADS)], axis=0)
        st_all = lax.dot_general(k_all, q_scr[i], (((1,), (1,)), ((), ())),
                                 preferred_element_type=F32)
        pt_parts = []
        for g in range(N_KV_HEADS):
            s_prev = st_all[2 * g * BLOCK:(2 * g + 1) * BLOCK, :]
            s_cur = st_all[(2 * g + 1) * BLOCK:(2 * g + 2) * BLOCK, :]
            if i == 0:
                s_prev = s_prev + neg_first
            s = jnp.where(from_prev, s_prev, s_cur)
            sink = sink_rows[g]
            m = jnp.maximum(jnp.max(s, axis=0, keepdims=True), sink)
            p = jnp.exp(s - m)
            den = jnp.sum(p, axis=0, keepdims=True) + jnp.exp(sink - m)
            pn = p * (1.0 / den)
            pt_parts.append(jnp.where(from_prev, pn, 0.0).astype(BF16))
            pt_parts.append(jnp.where(from_prev, 0.0, pn).astype(BF16))
        pt_all = jnp.concatenate(pt_parts, axis=0)
        vt_all = jnp.concatenate(
            [vt_scr[g, :, i * BLOCK:(i + 2) * BLOCK] for g in range(N_KV_HEADS)], axis=1)
        o_all = jnp.dot(vt_all, pt_all, preferred_element_type=F32).T
        for j in range(GROUP):
            attn_scr[i * BLOCK:(i + 1) * BLOCK, j * D_KV:(j + 1) * D_KV] = (
                o_all[j * BLOCK:(j + 1) * BLOCK, :].astype(BF16))

    y_attn = jnp.dot(attn_scr[...], wap_ref[...], preferred_element_type=F32)

    merged = (jax.nn.sigmoid(proj(OFF_ZC, D_MODEL)) * y_conv
              + jax.nn.sigmoid(proj(OFF_ZA, D_MODEL)) * y_attn).astype(BF16)
    o_ref[...] = x + gate * jnp.dot(merged, wout_ref[...], preferred_element_type=F32)
    _run_casts(cast_in, cast_out)


def _mix(x, mods, g, layer, cos_t, sin_t, conv_w, w_in, wqk, wcp, wap, wout, sinks, casts):
    bsz, seq, d = x.shape
    tm = TM_MIX
    nblk = tm // BLOCK
    n_s = seq // tm
    cin, cout, cshape = _cast_specs(casts, lambda b, s: b * n_s + s)
    outs = pl.pallas_call(
        functools.partial(_mix_kernel, layer=layer, n_cast=len(casts)),
        grid=(bsz, n_s),
        in_specs=[
            pl.BlockSpec(memory_space=pltpu.SMEM),
            pl.BlockSpec((None, tm, d), lambda b, s: (b, s, 0)),
            pl.BlockSpec((None, N_MOD, d), lambda b, s: (b, 0, 0)),
            _layer_spec((1, d), layer),
            pl.BlockSpec((tm, LANES), lambda b, s: (s, 0)),
            pl.BlockSpec((tm, LANES), lambda b, s: (s, 0)),
            _layer_spec((CONV_K, d), layer),
            _const_spec(w_in.shape),
            _const_spec(wqk.shape),
            _const_spec(wcp.shape),
            _const_spec(wap.shape),
            _const_spec(wout.shape),
        ] + cin,
        out_specs=[pl.BlockSpec((None, tm, d), lambda b, s: (b, s, 0))] + cout,
        scratch_shapes=[
            pltpu.VMEM((SUBLANES + tm, d), F32),
            pltpu.VMEM((N_KV_HEADS, BLOCK + tm, D_KV), BF16),
            pltpu.VMEM((N_KV_HEADS, D_KV, BLOCK + tm), BF16),
            pltpu.VMEM((nblk, GROUP * BLOCK, D_KV), BF16),
            pltpu.VMEM((tm, D_ATTN), BF16),
        ],
        out_shape=[jax.ShapeDtypeStruct(x.shape, F32)] + cshape,
        compiler_params=pltpu.CompilerParams(
            dimension_semantics=("arbitrary", "arbitrary"),
            vmem_limit_bytes=VMEM_LIMIT_BYTES),
        name="mixer",
    )(sinks, x, mods, g, cos_t, sin_t, conv_w, w_in, wqk, wcp, wap, wout,
      *[cst.src for cst in casts])
    return outs[0], outs[1:]


def _permuted_qk(w_in_l):
    d = w_in_l.shape[0]
    wq = w_in_l[:, OFF_Q:OFF_Q + D_ATTN].reshape(d, N_KV_HEADS, GROUP, 2, HALF)
    wq = wq.transpose(0, 2, 3, 1, 4).reshape(d, D_ATTN)
    wk = w_in_l[:, OFF_K:OFF_K + D_KV].reshape(d, N_KV_HEADS, 2, HALF)
    wk = wk.transpose(0, 2, 1, 3).reshape(d, D_KV)
    return jnp.concatenate([wq, wk], axis=1)


def _attn_proj_cast(w_attn_proj, layer, steps):
    rows, cols = w_attn_proj.shape[1:]
    blk = rows // steps
    per_head = HEAD_DIM // blk

    def in_index(t):
        chunk, within = t // per_head, t % per_head
        return ((chunk % N_KV_HEADS) * GROUP + chunk // N_KV_HEADS) * per_head + within, 0

    return _Cast(w_attn_proj, layer, (blk, cols), in_index, lambda t: (t, 0))


def kernel(x, c, w_ada, b_ada, g_ffn1, w1_gu, w1_down, g_mix, w_in, conv_w,
           w_conv_proj, w_attn_proj, sinks, w_out, g_ffn2, w2_gu, w2_down, g_final):
    bsz, seq, d = x.shape
    depth = w_ada.shape[0]
    ffn_steps = bsz * (seq // TM_FFN)
    mix_steps = bsz * (seq // TM_MIX)

    inv = 1.0 / (ROPE_THETA ** (jnp.arange(0, HEAD_DIM, 2, dtype=F32) / HEAD_DIM))
    ang = jnp.arange(seq, dtype=F32)[:, None] * inv[None, :]
    cos_t = jnp.tile(jnp.cos(ang), (1, LANES // HALF))
    sin_t = jnp.tile(jnp.sin(ang), (1, LANES // HALF))

    c_pad = jnp.pad(c, ((0, SUBLANES - bsz % SUBLANES), (0, 0)))
    gf = g_final.reshape(1, d)
    b_ada3 = b_ada.reshape(depth, 1, -1)
    g1, gm, g2 = (t.reshape(depth, 1, d) for t in (g_ffn1, g_mix, g_ffn2))
    wqk_f32 = jnp.stack([_permuted_qk(w_in[l]) for l in range(depth)])

    for l in range(depth):
        mods, (wgu1, wd1) = _ada(
            c_pad, w_ada, b_ada3, l,
            [_grid_cast(w1_gu, l, ADA_STEPS), _grid_cast(w1_down, l, ADA_STEPS)])
        mods = mods[:bsz].reshape(bsz, N_MOD, d)

        x, (win, wqk, wcp, wap, wout) = _ffn(
            x, mods, g1, l, wgu1, wd1, gf,
            [_grid_cast(w_in, l, ffn_steps), _grid_cast(wqk_f32, l, ffn_steps),
             _grid_cast(w_conv_proj, l, ffn_steps), _attn_proj_cast(w_attn_proj, l, ffn_steps),
             _grid_cast(w_out, l, ffn_steps)],
            mod_base=0, final=False)

        x, (wgu2, wd2) = _mix(
            x, mods, gm, l, cos_t, sin_t, conv_w, win, wqk, wcp, wap, wout, sinks,
            [_grid_cast(w2_gu, l, mix_steps), _grid_cast(w2_down, l, mix_steps)])

        x, _ = _ffn(x, mods, g2, l, wgu2, wd2, gf, [], mod_base=6, final=(l == depth - 1))
    return x
```

```python
import functools
from typing import Callable, NamedTuple

import jax
import jax.numpy as jnp
from jax import lax
from jax.experimental import pallas as pl
from jax.experimental.pallas import tpu as pltpu

D_MODEL = 1024
CONV_K = 3
HEAD_DIM = 64
HALF = HEAD_DIM // 2
N_HEADS = 16
N_KV_HEADS = 4
GROUP = N_HEADS // N_KV_HEADS
D_ATTN = N_HEADS * HEAD_DIM
D_KV = N_KV_HEADS * HEAD_DIM
WINDOW = 128
BLOCK = 128
ROPE_THETA = 10000.0
D_FF = 2816
N_MOD = 9
EPS = 1e-6
NEG_INF = -1e30

LANES = 128
SUBLANES = 8
BF16_SUBLANES = 16
VMEM_LIMIT_BYTES = 56 * 1024 * 1024

TM_FFN = 512
TM_MIX = 512
ADA_STEPS = 8

OFF_B, OFF_C, OFF_U = 0, D_MODEL, 2 * D_MODEL
OFF_Q = 3 * D_MODEL
OFF_K = OFF_Q + D_ATTN
OFF_V = OFF_K + D_KV
OFF_ZC = OFF_V + D_KV
OFF_ZA = OFF_ZC + D_MODEL

BF16 = jnp.bfloat16
F32 = jnp.float32


def _const_spec(shape):
    nd = len(shape)
    return pl.BlockSpec(shape, lambda *_: (0,) * nd, pipeline_mode=pl.Buffered(1))


def _layer_spec(shape, layer):
    nd = len(shape)
    return pl.BlockSpec((None,) + tuple(shape), lambda *_: (layer,) + (0,) * nd,
                        pipeline_mode=pl.Buffered(1))


class _Cast(NamedTuple):
    src: jax.Array
    layer: int
    block: tuple
    in_index: Callable
    out_index: Callable
    out_cols: int
    pick: Callable


def _grid_cast(src, layer, steps):
    rows, cols = src.shape[1:]
    col_blocks = 1
    while (rows * col_blocks) % (steps * BF16_SUBLANES):
        col_blocks *= 2
    row_blocks = steps // col_blocks
    idx = lambda t: (t // col_blocks, t % col_blocks)
    return _Cast(src, layer, (rows // row_blocks, cols // col_blocks), idx, idx, cols,
                 lambda blk: blk)


def _cast_specs(casts, linear_step):
    in_specs, out_specs, out_shapes = [], [], []
    for cst in casts:
        col_blocks = cst.src.shape[2] // cst.block[1]
        in_specs.append(pl.BlockSpec(
            (None,) + cst.block,
            lambda *idx, cst=cst: (cst.layer,) + cst.in_index(linear_step(*idx))))
        out_specs.append(pl.BlockSpec(
            (cst.block[0], cst.out_cols // col_blocks),
            lambda *idx, cst=cst: cst.out_index(linear_step(*idx))))
        out_shapes.append(jax.ShapeDtypeStruct((cst.src.shape[1], cst.out_cols), BF16))
    return in_specs, out_specs, out_shapes


def _run_casts(picks, src_refs, dst_refs):
    for pick, src, dst in zip(picks, src_refs, dst_refs):
        dst[...] = pick(src[...]).astype(BF16)


def _rms_mod(x, g, shift, scale):
    ms = jnp.mean(x * x, axis=-1, keepdims=True)
    y = x * lax.rsqrt(ms + EPS) * g
    return y * (1.0 + scale) + shift


def _ada_kernel(*refs, picks):
    n_cast = len(picks)
    c_ref, w_ref, b_ref = refs[:3]
    cast_in = refs[3:3 + n_cast]
    o_ref = refs[3 + n_cast]
    cast_out = refs[4 + n_cast:]
    c = c_ref[...]
    ca = (c * jax.nn.sigmoid(c)).astype(BF16)
    o_ref[...] = jnp.dot(ca, w_ref[...].astype(BF16),
                         preferred_element_type=F32) + b_ref[...]
    _run_casts(picks, cast_in, cast_out)


def _ada(c_pad, w_ada, b_ada, layer, casts):
    rows = c_pad.shape[0]
    n = w_ada.shape[2]
    tn = n // ADA_STEPS
    cin, cout, cshape = _cast_specs(casts, lambda j: j)
    outs = pl.pallas_call(
        functools.partial(_ada_kernel, picks=tuple(cst.pick for cst in casts)),
        grid=(ADA_STEPS,),
        in_specs=[
            pl.BlockSpec((rows, D_MODEL), lambda j: (0, 0)),
            pl.BlockSpec((None, D_MODEL, tn), lambda j: (layer, 0, j)),
            pl.BlockSpec((None, 1, tn), lambda j: (layer, 0, j)),
        ] + cin,
        out_specs=[pl.BlockSpec((rows, tn), lambda j: (0, j))] + cout,
        out_shape=[jax.ShapeDtypeStruct((rows, n), F32)] + cshape,
        compiler_params=pltpu.CompilerParams(
            dimension_semantics=("arbitrary",),
            vmem_limit_bytes=VMEM_LIMIT_BYTES),
        name="ada_mod",
    )(c_pad, w_ada, b_ada, *[cst.src for cst in casts])
    return outs[0], outs[1:]


def _ffn_kernel(*refs, mod_base, final, picks):
    n_cast = len(picks)
    x_ref, mod_ref, g_ref, wgu_ref, wd_ref, gf_ref = refs[:6]
    cast_in = refs[6:6 + n_cast]
    o_ref = refs[6 + n_cast]
    cast_out = refs[7 + n_cast:]
    x = x_ref[...]
    shift = mod_ref[mod_base:mod_base + 1, :]
    scale = mod_ref[mod_base + 1:mod_base + 2, :]
    gate = mod_ref[mod_base + 2:mod_base + 3, :]
    h = _rms_mod(x, g_ref[...], shift, scale).astype(BF16)
    gu = jnp.dot(h, wgu_ref[...], preferred_element_type=F32)
    a = gu[:, :D_FF]
    b = gu[:, D_FF:]
    act = (a * jax.nn.sigmoid(a) * b).astype(BF16)
    f = jnp.dot(act, wd_ref[...], preferred_element_type=F32)
    xn = x + (0.5 * gate) * f
    if final:
        ms = jnp.mean(xn * xn, axis=-1, keepdims=True)
        xn = xn * lax.rsqrt(ms + EPS) * gf_ref[...]
    o_ref[...] = xn
    _run_casts(picks, cast_in, cast_out)


def _ffn(x, mods, g, layer, wgu, wd, g_final, casts, *, mod_base, final):
    bsz, seq, d = x.shape
    n_s = seq // TM_FFN
    cin, cout, cshape = _cast_specs(casts, lambda b, s: b * n_s + s)
    kern = functools.partial(_ffn_kernel, mod_base=mod_base, final=final,
                             picks=tuple(cst.pick for cst in casts))
    outs = pl.pallas_call(
        kern,
        grid=(bsz, n_s),
        in_specs=[
            pl.BlockSpec((None, TM_FFN, d), lambda b, s: (b, s, 0)),
            pl.BlockSpec((None, N_MOD, d), lambda b, s: (b, 0, 0)),
            _layer_spec((1, d), layer),
            _const_spec(wgu.shape),
            _const_spec(wd.shape),
            _const_spec((1, d)),
        ] + cin,
        out_specs=[pl.BlockSpec((None, TM_FFN, d), lambda b, s: (b, s, 0))] + cout,
        out_shape=[jax.ShapeDtypeStruct(x.shape, F32)] + cshape,
        compiler_params=pltpu.CompilerParams(
            dimension_semantics=("arbitrary", "arbitrary"),
            vmem_limit_bytes=VMEM_LIMIT_BYTES),
        name="ffn_final" if final else "ffn",
    )(x, mods, g, wgu, wd, g_final, *[cst.src for cst in casts])
    return outs[0], outs[1:]


def _mix_kernel(*refs, layer, picks):
    n_cast = len(picks)
    (sinks_ref, x_ref, mod_ref, g_ref, cos_ref, sin_ref, convw_ref,
     win_ref, wqk_ref, wcp_ref, wap_ref, wout_ref) = refs[:12]
    cast_in = refs[12:12 + n_cast]
    o_ref = refs[12 + n_cast]
    cast_out = refs[13 + n_cast:13 + 2 * n_cast]
    cu_scr, k_scr, vt_scr, q_scr, attn_scr = refs[13 + 2 * n_cast:]
    tm = x_ref.shape[0]
    nblk = tm // BLOCK
    s_idx = pl.program_id(1)

    @pl.when(s_idx == 0)
    def _():
        cu_scr[0:SUBLANES, :] = jnp.zeros((SUBLANES, D_MODEL), F32)
        k_scr[:, 0:BLOCK, :] = jnp.zeros((N_KV_HEADS, BLOCK, D_KV), BF16)
        vt_scr[:, :, 0:BLOCK] = jnp.zeros((N_KV_HEADS, D_KV, BLOCK), BF16)

    @pl.when(s_idx > 0)
    def _():
        cu_scr[0:SUBLANES, :] = cu_scr[tm:tm + SUBLANES, :]
        for g in range(N_KV_HEADS):
            k_scr[g, 0:BLOCK, :] = k_scr[g, tm:tm + BLOCK, :]
            vt_scr[g, :, 0:BLOCK] = vt_scr[g, :, tm:tm + BLOCK]

    x = x_ref[...]
    shift = mod_ref[3:4, :]
    scale = mod_ref[4:5, :]
    gate = mod_ref[5:6, :]
    h = _rms_mod(x, g_ref[...], shift, scale).astype(BF16)

    def proj(off, width):
        return jnp.dot(h, win_ref[:, off:off + width], preferred_element_type=F32)

    cu = proj(OFF_C, D_MODEL) * proj(OFF_U, D_MODEL)
    cu_scr[SUBLANES:SUBLANES + tm, :] = cu
    conv = (cu_scr[SUBLANES - 2:SUBLANES - 2 + tm, :] * convw_ref[0:1, :]
            + cu_scr[SUBLANES - 1:SUBLANES - 1 + tm, :] * convw_ref[1:2, :]
            + cu * convw_ref[2:3, :])
    bconv = (proj(OFF_B, D_MODEL) * conv).astype(BF16)
    y_conv = jnp.dot(bconv, wcp_ref[...], preferred_element_type=F32)

    cos = cos_ref[...]
    sin = sin_ref[...]
    lane = lax.broadcasted_iota(jnp.int32, (1, D_KV), 1)
    head_of_lane = (lane % LANES) // HALF
    vrow = lax.broadcasted_iota(jnp.int32, (D_KV, 1), 0)
    head_of_vrow = vrow // HEAD_DIM

    k = jnp.dot(h, wqk_ref[:, D_ATTN:], preferred_element_type=F32)
    k1, k2 = k[:, :LANES], k[:, LANES:]
    kr = jnp.concatenate([k1 * cos - k2 * sin, k2 * cos + k1 * sin], axis=-1)
    vt = proj(OFF_V, D_KV).T
    for g in range(N_KV_HEADS):
        k_scr[g, BLOCK:BLOCK + tm, :] = jnp.where(head_of_lane == g, kr, 0.0).astype(BF16)
        vt_scr[g, :, BLOCK:BLOCK + tm] = jnp.where(head_of_vrow == g, vt, 0.0).astype(BF16)

    q = jnp.dot(h, wqk_ref[:, :D_ATTN], preferred_element_type=F32)
    qscale = HEAD_DIM ** -0.5
    for j in range(GROUP):
        q1 = q[:, j * D_KV:j * D_KV + LANES]
        q2 = q[:, j * D_KV + LANES:(j + 1) * D_KV]
        qr = jnp.concatenate([(q1 * cos - q2 * sin) * qscale,
                              (q2 * cos + q1 * sin) * qscale], axis=-1).astype(BF16)
        for i in range(nblk):
            q_scr[i, j * BLOCK:(j + 1) * BLOCK, :] = qr[i * BLOCK:(i + 1) * BLOCK, :]

    key_slot = lax.broadcasted_iota(jnp.int32, (BLOCK, GROUP * BLOCK), 0)
    query = lax.broadcasted_iota(jnp.int32, (BLOCK, GROUP * BLOCK), 1)
    from_prev = key_slot > (query % BLOCK)
    neg_first = jnp.where(s_idx == 0, NEG_INF, 0.0)
    sink_rows = [
        jnp.concatenate([jnp.full((1, BLOCK), sinks_ref[layer, g * GROUP + j], F32)
                         for j in range(GROUP)], axis=1)
        for g in range(N_KV_HEADS)]

    for i in range(nblk):
        k_all = jnp.concatenate(
            [k_scr[g, i * BLOCK:(i + 2) * BLOCK, :] for g in range(N_KV_HEADS)], axis=0)
        st_all = lax.dot_general(k_all, q_scr[i], (((1,), (1,)), ((), ())),
                                 preferred_element_type=F32)
        pt_parts = []
        for g in range(N_KV_HEADS):
            s_prev = st_all[2 * g * BLOCK:(2 * g + 1) * BLOCK, :]
            s_cur = st_all[(2 * g + 1) * BLOCK:(2 * g + 2) * BLOCK, :]
            if i == 0:
                s_prev = s_prev + neg_first
            s = jnp.where(from_prev, s_prev, s_cur)
            sink = sink_rows[g]
            m = jnp.maximum(jnp.max(s, axis=0, keepdims=True), sink)
            p = jnp.exp(s - m)
            den = jnp.sum(p, axis=0, keepdims=True) + jnp.exp(sink - m)
            pn = p * (1.0 / den)
            pt_parts.append(jnp.where(from_prev, pn, 0.0).astype(BF16))
            pt_parts.append(jnp.where(from_prev, 0.0, pn).astype(BF16))
        pt_all = jnp.concatenate(pt_parts, axis=0)
        vt_all = jnp.concatenate(
            [vt_scr[g, :, i * BLOCK:(i + 2) * BLOCK] for g in range(N_KV_HEADS)], axis=1)
        o_all = jnp.dot(vt_all, pt_all, preferred_element_type=F32).T
        for j in range(GROUP):
            attn_scr[i * BLOCK:(i + 1) * BLOCK, j * D_KV:(j + 1) * D_KV] = (
                o_all[j * BLOCK:(j + 1) * BLOCK, :].astype(BF16))

    y_attn = jnp.dot(attn_scr[...], wap_ref[...], preferred_element_type=F32)

    merged = (jax.nn.sigmoid(proj(OFF_ZC, D_MODEL)) * y_conv
              + jax.nn.sigmoid(proj(OFF_ZA, D_MODEL)) * y_attn).astype(BF16)
    o_ref[...] = x + gate * jnp.dot(merged, wout_ref[...], preferred_element_type=F32)
    _run_casts(picks, cast_in, cast_out)


def _mix(x, mods, g, layer, cos_t, sin_t, conv_w, w_in, wqk, wcp, wap, wout, sinks, casts):
    bsz, seq, d = x.shape
    tm = TM_MIX
    nblk = tm // BLOCK
    n_s = seq // tm
    cin, cout, cshape = _cast_specs(casts, lambda b, s: b * n_s + s)
    outs = pl.pallas_call(
        functools.partial(_mix_kernel, layer=layer, picks=tuple(cst.pick for cst in casts)),
        grid=(bsz, n_s),
        in_specs=[
            pl.BlockSpec(memory_space=pltpu.SMEM),
            pl.BlockSpec((None, tm, d), lambda b, s: (b, s, 0)),
            pl.BlockSpec((None, N_MOD, d), lambda b, s: (b, 0, 0)),
            _layer_spec((1, d), layer),
            pl.BlockSpec((tm, LANES), lambda b, s: (s, 0)),
            pl.BlockSpec((tm, LANES), lambda b, s: (s, 0)),
            _layer_spec((CONV_K, d), layer),
            _const_spec(w_in.shape),
            _const_spec(wqk.shape),
            _const_spec(wcp.shape),
            _const_spec(wap.shape),
            _const_spec(wout.shape),
        ] + cin,
        out_specs=[pl.BlockSpec((None, tm, d), lambda b, s: (b, s, 0))] + cout,
        scratch_shapes=[
            pltpu.VMEM((SUBLANES + tm, d), F32),
            pltpu.VMEM((N_KV_HEADS, BLOCK + tm, D_KV), BF16),
            pltpu.VMEM((N_KV_HEADS, D_KV, BLOCK + tm), BF16),
            pltpu.VMEM((nblk, GROUP * BLOCK, D_KV), BF16),
            pltpu.VMEM((tm, D_ATTN), BF16),
        ],
        out_shape=[jax.ShapeDtypeStruct(x.shape, F32)] + cshape,
        compiler_params=pltpu.CompilerParams(
            dimension_semantics=("arbitrary", "arbitrary"),
            vmem_limit_bytes=VMEM_LIMIT_BYTES),
        name="mixer",
    )(sinks, x, mods, g, cos_t, sin_t, conv_w, w_in, wqk, wcp, wap, wout,
      *[cst.src for cst in casts])
    return outs[0], outs[1:]


def _pick_permuted_qk(w_rows):
    pieces = []
    for j in range(GROUP):
        for half in range(2):
            for g in range(N_KV_HEADS):
                start = OFF_Q + (g * GROUP + j) * HEAD_DIM + half * HALF
                pieces.append(w_rows[:, start:start + HALF])
    for half in range(2):
        for g in range(N_KV_HEADS):
            start = OFF_K + g * HEAD_DIM + half * HALF
            pieces.append(w_rows[:, start:start + HALF])
    return jnp.concatenate(pieces, axis=1)


def _qk_cast(w_in, layer, steps):
    rows, cols = w_in.shape[1:]
    idx = lambda t: (t, 0)
    return _Cast(w_in, layer, (rows // steps, cols), idx, idx, D_ATTN + D_KV, _pick_permuted_qk)


def _attn_proj_cast(w_attn_proj, layer, steps):
    rows, cols = w_attn_proj.shape[1:]
    blk = rows // steps
    per_head = HEAD_DIM // blk

    def in_index(t):
        chunk, within = t // per_head, t % per_head
        return ((chunk % N_KV_HEADS) * GROUP + chunk // N_KV_HEADS) * per_head + within, 0

    return _Cast(w_attn_proj, layer, (blk, cols), in_index, lambda t: (t, 0), cols,
                 lambda blk_rows: blk_rows)


def kernel(x, c, w_ada, b_ada, g_ffn1, w1_gu, w1_down, g_mix, w_in, conv_w,
           w_conv_proj, w_attn_proj, sinks, w_out, g_ffn2, w2_gu, w2_down, g_final):
    bsz, seq, d = x.shape
    depth = w_ada.shape[0]
    ffn_steps = bsz * (seq // TM_FFN)
    mix_steps = bsz * (seq // TM_MIX)

    inv = 1.0 / (ROPE_THETA ** (jnp.arange(0, HEAD_DIM, 2, dtype=F32) / HEAD_DIM))
    ang = jnp.arange(seq, dtype=F32)[:, None] * inv[None, :]
    cos_t = jnp.tile(jnp.cos(ang), (1, LANES // HALF))
    sin_t = jnp.tile(jnp.sin(ang), (1, LANES // HALF))

    c_pad = jnp.pad(c, ((0, SUBLANES - bsz % SUBLANES), (0, 0)))
    gf = g_final.reshape(1, d)
    b_ada3 = b_ada.reshape(depth, 1, -1)
    g1, gm, g2 = (t.reshape(depth, 1, d) for t in (g_ffn1, g_mix, g_ffn2))

    for l in range(depth):
        mods, (wgu1, wd1) = _ada(
            c_pad, w_ada, b_ada3, l,
            [_grid_cast(w1_gu, l, ADA_STEPS), _grid_cast(w1_down, l, ADA_STEPS)])
        mods = mods[:bsz].reshape(bsz, N_MOD, d)

        x, (win, wqk, wcp, wap, wout) = _ffn(
            x, mods, g1, l, wgu1, wd1, gf,
            [_grid_cast(w_in, l, ffn_steps), _qk_cast(w_in, l, ffn_steps),
             _grid_cast(w_conv_proj, l, ffn_steps), _attn_proj_cast(w_attn_proj, l, ffn_steps),
             _grid_cast(w_out, l, ffn_steps)],
            mod_base=0, final=False)

        x, (wgu2, wd2) = _mix(
            x, mods, gm, l, cos_t, sin_t, conv_w, win, wqk, wcp, wap, wout, sinks,
            [_grid_cast(w2_gu, l, mix_steps), _grid_cast(w2_down, l, mix_steps)])

        x, _ = _ffn(x, mods, g2, l, wgu2, wd2, gf, [], mod_base=6, final=(l == depth - 1))
    return x
```

```python
import functools
from typing import Callable, NamedTuple

import jax
import jax.numpy as jnp
from jax import lax
from jax.experimental import pallas as pl
from jax.experimental.pallas import tpu as pltpu

D_MODEL = 1024
CONV_K = 3
HEAD_DIM = 64
HALF = HEAD_DIM // 2
N_HEADS = 16
N_KV_HEADS = 4
GROUP = N_HEADS // N_KV_HEADS
D_ATTN = N_HEADS * HEAD_DIM
D_KV = N_KV_HEADS * HEAD_DIM
WINDOW = 128
BLOCK = 128
ROPE_THETA = 10000.0
D_FF = 2816
N_MOD = 9
EPS = 1e-6
NEG_INF = -1e30

LANES = 128
SUBLANES = 8
BF16_SUBLANES = 16
VMEM_LIMIT_BYTES = 56 * 1024 * 1024

TM_FFN = 1024
FFN_SUBTILES = 8
TM_MIX = 512
ADA_STEPS = 8

OFF_B, OFF_C, OFF_U = 0, D_MODEL, 2 * D_MODEL
OFF_Q = 3 * D_MODEL
OFF_K = OFF_Q + D_ATTN
OFF_V = OFF_K + D_KV
OFF_ZC = OFF_V + D_KV
OFF_ZA = OFF_ZC + D_MODEL

BF16 = jnp.bfloat16
F32 = jnp.float32


def _const_spec(shape):
    nd = len(shape)
    return pl.BlockSpec(shape, lambda *_: (0,) * nd, pipeline_mode=pl.Buffered(1))


def _layer_spec(shape, layer):
    nd = len(shape)
    return pl.BlockSpec((None,) + tuple(shape), lambda *_: (layer,) + (0,) * nd,
                        pipeline_mode=pl.Buffered(1))


class _Cast(NamedTuple):
    src: jax.Array
    layer: int
    block: tuple
    in_index: Callable
    out_index: Callable
    out_cols: int
    pick: Callable


def _grid_cast(src, layer, steps):
    rows, cols = src.shape[1:]
    col_blocks = 1
    while (rows * col_blocks) % (steps * BF16_SUBLANES):
        col_blocks *= 2
    row_blocks = steps // col_blocks
    idx = lambda t: (t // col_blocks, t % col_blocks)
    return _Cast(src, layer, (rows // row_blocks, cols // col_blocks), idx, idx, cols,
                 lambda blk: blk)


def _cast_specs(casts, linear_step):
    in_specs, out_specs, out_shapes = [], [], []
    for cst in casts:
        col_blocks = cst.src.shape[2] // cst.block[1]
        in_specs.append(pl.BlockSpec(
            (None,) + cst.block,
            lambda *idx, cst=cst: (cst.layer,) + cst.in_index(linear_step(*idx))))
        out_specs.append(pl.BlockSpec(
            (cst.block[0], cst.out_cols // col_blocks),
            lambda *idx, cst=cst: cst.out_index(linear_step(*idx))))
        out_shapes.append(jax.ShapeDtypeStruct((cst.src.shape[1], cst.out_cols), BF16))
    return in_specs, out_specs, out_shapes


def _run_casts(picks, src_refs, dst_refs):
    for pick, src, dst in zip(picks, src_refs, dst_refs):
        dst[...] = pick(src[...]).astype(BF16)


def _rms_mod(x, g, shift, scale):
    ms = jnp.mean(x * x, axis=-1, keepdims=True)
    y = x * lax.rsqrt(ms + EPS) * g
    return y * (1.0 + scale) + shift


def _ada_kernel(*refs, picks):
    n_cast = len(picks)
    c_ref, w_ref, b_ref = refs[:3]
    cast_in = refs[3:3 + n_cast]
    o_ref = refs[3 + n_cast]
    cast_out = refs[4 + n_cast:]
    c = c_ref[...]
    ca = (c * jax.nn.sigmoid(c)).astype(BF16)
    o_ref[...] = jnp.dot(ca, w_ref[...].astype(BF16),
                         preferred_element_type=F32) + b_ref[...]
    _run_casts(picks, cast_in, cast_out)


def _ada(c_pad, w_ada, b_ada, layer, casts):
    rows = c_pad.shape[0]
    n = w_ada.shape[2]
    tn = n // ADA_STEPS
    cin, cout, cshape = _cast_specs(casts, lambda j: j)
    outs = pl.pallas_call(
        functools.partial(_ada_kernel, picks=tuple(cst.pick for cst in casts)),
        grid=(ADA_STEPS,),
        in_specs=[
            pl.BlockSpec((rows, D_MODEL), lambda j: (0, 0)),
            pl.BlockSpec((None, D_MODEL, tn), lambda j: (layer, 0, j)),
            pl.BlockSpec((None, 1, tn), lambda j: (layer, 0, j)),
        ] + cin,
        out_specs=[pl.BlockSpec((rows, tn), lambda j: (0, j))] + cout,
        out_shape=[jax.ShapeDtypeStruct((rows, n), F32)] + cshape,
        compiler_params=pltpu.CompilerParams(
            dimension_semantics=("arbitrary",),
            vmem_limit_bytes=VMEM_LIMIT_BYTES),
        name="ada_mod",
    )(c_pad, w_ada, b_ada, *[cst.src for cst in casts])
    return outs[0], outs[1:]


def _ffn_kernel(*refs, mod_base, final, picks):
    n_cast = len(picks)
    x_ref, mod_ref, g_ref, wgu_ref, wd_ref, gf_ref = refs[:6]
    cast_in = refs[6:6 + n_cast]
    o_ref = refs[6 + n_cast]
    cast_out = refs[7 + n_cast:]
    shift = mod_ref[mod_base:mod_base + 1, :]
    scale = mod_ref[mod_base + 1:mod_base + 2, :]
    gate = mod_ref[mod_base + 2:mod_base + 3, :]
    rows = x_ref.shape[0] // FFN_SUBTILES
    for r in range(FFN_SUBTILES):
        x = x_ref[r * rows:(r + 1) * rows, :]
        h = _rms_mod(x, g_ref[...], shift, scale).astype(BF16)
        gu = jnp.dot(h, wgu_ref[...], preferred_element_type=F32)
        a = gu[:, :D_FF]
        b = gu[:, D_FF:]
        act = (a * jax.nn.sigmoid(a) * b).astype(BF16)
        f = jnp.dot(act, wd_ref[...], preferred_element_type=F32)
        xn = x + (0.5 * gate) * f
        if final:
            ms = jnp.mean(xn * xn, axis=-1, keepdims=True)
            xn = xn * lax.rsqrt(ms + EPS) * gf_ref[...]
        o_ref[r * rows:(r + 1) * rows, :] = xn
    _run_casts(picks, cast_in, cast_out)


def _ffn(x, mods, g, layer, wgu, wd, g_final, casts, *, mod_base, final):
    bsz, seq, d = x.shape
    n_s = seq // TM_FFN
    cin, cout, cshape = _cast_specs(casts, lambda b, s: b * n_s + s)
    kern = functools.partial(_ffn_kernel, mod_base=mod_base, final=final,
                             picks=tuple(cst.pick for cst in casts))
    outs = pl.pallas_call(
        kern,
        grid=(bsz, n_s),
        in_specs=[
            pl.BlockSpec((None, TM_FFN, d), lambda b, s: (b, s, 0)),
            pl.BlockSpec((None, N_MOD, d), lambda b, s: (b, 0, 0)),
            _layer_spec((1, d), layer),
            _const_spec(wgu.shape),
            _const_spec(wd.shape),
            _const_spec((1, d)),
        ] + cin,
        out_specs=[pl.BlockSpec((None, TM_FFN, d), lambda b, s: (b, s, 0))] + cout,
        out_shape=[jax.ShapeDtypeStruct(x.shape, F32)] + cshape,
        compiler_params=pltpu.CompilerParams(
            dimension_semantics=("arbitrary", "arbitrary"),
            vmem_limit_bytes=VMEM_LIMIT_BYTES),
        name="ffn_final" if final else "ffn",
    )(x, mods, g, wgu, wd, g_final, *[cst.src for cst in casts])
    return outs[0], outs[1:]


def _mix_kernel(*refs, layer, picks):
    n_cast = len(picks)
    (sinks_ref, x_ref, mod_ref, g_ref, cos_ref, sin_ref, convw_ref,
     win_ref, wqk_ref, wcp_ref, wap_ref, wout_ref) = refs[:12]
    cast_in = refs[12:12 + n_cast]
    o_ref = refs[12 + n_cast]
    cast_out = refs[13 + n_cast:13 + 2 * n_cast]
    cu_scr, k_scr, vt_scr, q_scr, attn_scr = refs[13 + 2 * n_cast:]
    tm = x_ref.shape[0]
    nblk = tm // BLOCK
    s_idx = pl.program_id(1)

    @pl.when(s_idx == 0)
    def _():
        cu_scr[0:SUBLANES, :] = jnp.zeros((SUBLANES, D_MODEL), F32)
        k_scr[:, 0:BLOCK, :] = jnp.zeros((N_KV_HEADS, BLOCK, D_KV), BF16)
        vt_scr[:, :, 0:BLOCK] = jnp.zeros((N_KV_HEADS, D_KV, BLOCK), BF16)

    @pl.when(s_idx > 0)
    def _():
        cu_scr[0:SUBLANES, :] = cu_scr[tm:tm + SUBLANES, :]
        for g in range(N_KV_HEADS):
            k_scr[g, 0:BLOCK, :] = k_scr[g, tm:tm + BLOCK, :]
            vt_scr[g, :, 0:BLOCK] = vt_scr[g, :, tm:tm + BLOCK]

    x = x_ref[...]
    shift = mod_ref[3:4, :]
    scale = mod_ref[4:5, :]
    gate = mod_ref[5:6, :]
    h = _rms_mod(x, g_ref[...], shift, scale).astype(BF16)

    def proj(off, width):
        return jnp.dot(h, win_ref[:, off:off + width], preferred_element_type=F32)

    cu = proj(OFF_C, D_MODEL) * proj(OFF_U, D_MODEL)
    cu_scr[SUBLANES:SUBLANES + tm, :] = cu
    conv = (cu_scr[SUBLANES - 2:SUBLANES - 2 + tm, :] * convw_ref[0:1, :]
            + cu_scr[SUBLANES - 1:SUBLANES - 1 + tm, :] * convw_ref[1:2, :]
            + cu * convw_ref[2:3, :])
    bconv = (proj(OFF_B, D_MODEL) * conv).astype(BF16)
    y_conv = jnp.dot(bconv, wcp_ref[...], preferred_element_type=F32)

    cos = cos_ref[...]
    sin = sin_ref[...]
    lane = lax.broadcasted_iota(jnp.int32, (1, D_KV), 1)
    head_of_lane = (lane % LANES) // HALF
    vrow = lax.broadcasted_iota(jnp.int32, (D_KV, 1), 0)
    head_of_vrow = vrow // HEAD_DIM

    k = jnp.dot(h, wqk_ref[:, D_ATTN:], preferred_element_type=F32)
    k1, k2 = k[:, :LANES], k[:, LANES:]
    kr = jnp.concatenate([k1 * cos - k2 * sin, k2 * cos + k1 * sin], axis=-1)
    vt = proj(OFF_V, D_KV).T
    for g in range(N_KV_HEADS):
        k_scr[g, BLOCK:BLOCK + tm, :] = jnp.where(head_of_lane == g, kr, 0.0).astype(BF16)
        vt_scr[g, :, BLOCK:BLOCK + tm] = jnp.where(head_of_vrow == g, vt, 0.0).astype(BF16)

    q = jnp.dot(h, wqk_ref[:, :D_ATTN], preferred_element_type=F32)
    qscale = HEAD_DIM ** -0.5
    for j in range(GROUP):
        q1 = q[:, j * D_KV:j * D_KV + LANES]
        q2 = q[:, j * D_KV + LANES:(j + 1) * D_KV]
        qr = jnp.concatenate([(q1 * cos - q2 * sin) * qscale,
                              (q2 * cos + q1 * sin) * qscale], axis=-1).astype(BF16)
        for i in range(nblk):
            q_scr[i, j * BLOCK:(j + 1) * BLOCK, :] = qr[i * BLOCK:(i + 1) * BLOCK, :]

    key_slot = lax.broadcasted_iota(jnp.int32, (BLOCK, GROUP * BLOCK), 0)
    query = lax.broadcasted_iota(jnp.int32, (BLOCK, GROUP * BLOCK), 1)
    from_prev = key_slot > (query % BLOCK)
    neg_first = jnp.where(s_idx == 0, NEG_INF, 0.0)
    sink_rows = [
        jnp.concatenate([jnp.full((1, BLOCK), sinks_ref[layer, g * GROUP + j], F32)
                         for j in range(GROUP)], axis=1)
        for g in range(N_KV_HEADS)]

    for i in range(nblk):
        k_all = jnp.concatenate(
            [k_scr[g, i * BLOCK:(i + 2) * BLOCK, :] for g in range(N_KV_HEADS)], axis=0)
        st_all = lax.dot_general(k_all, q_scr[i], (((1,), (1,)), ((), ())),
                                 preferred_element_type=F32)
        pt_parts = []
        for g in range(N_KV_HEADS):
            s_prev = st_all[2 * g * BLOCK:(2 * g + 1) * BLOCK, :]
            s_cur = st_all[(2 * g + 1) * BLOCK:(2 * g + 2) * BLOCK, :]
            if i == 0:
                s_prev = s_prev + neg_first
            s = jnp.where(from_prev, s_prev, s_cur)
            sink = sink_rows[g]
            m = jnp.maximum(jnp.max(s, axis=0, keepdims=True), sink)
            p = jnp.exp(s - m)
            den = jnp.sum(p, axis=0, keepdims=True) + jnp.exp(sink - m)
            pn = p * (1.0 / den)
            pt_parts.append(jnp.where(from_prev, pn, 0.0).astype(BF16))
            pt_parts.append(jnp.where(from_prev, 0.0, pn).astype(BF16))
        pt_all = jnp.concatenate(pt_parts, axis=0)
        vt_all = jnp.concatenate(
            [vt_scr[g, :, i * BLOCK:(i + 2) * BLOCK] for g in range(N_KV_HEADS)], axis=1)
        o_all = jnp.dot(vt_all, pt_all, preferred_element_type=F32).T
        for j in range(GROUP):
            attn_scr[i * BLOCK:(i + 1) * BLOCK, j * D_KV:(j + 1) * D_KV] = (
                o_all[j * BLOCK:(j + 1) * BLOCK, :].astype(BF16))

    y_attn = jnp.dot(attn_scr[...], wap_ref[...], preferred_element_type=F32)

    merged = (jax.nn.sigmoid(proj(OFF_ZC, D_MODEL)) * y_conv
              + jax.nn.sigmoid(proj(OFF_ZA, D_MODEL)) * y_attn).astype(BF16)
    o_ref[...] = x + gate * jnp.dot(merged, wout_ref[...], preferred_element_type=F32)
    _run_casts(picks, cast_in, cast_out)


def _mix(x, mods, g, layer, cos_t, sin_t, conv_w, w_in, wqk, wcp, wap, wout, sinks, casts):
    bsz, seq, d = x.shape
    tm = TM_MIX
    nblk = tm // BLOCK
    n_s = seq // tm
    cin, cout, cshape = _cast_specs(casts, lambda b, s: b * n_s + s)
    outs = pl.pallas_call(
        functools.partial(_mix_kernel, layer=layer, picks=tuple(cst.pick for cst in casts)),
        grid=(bsz, n_s),
        in_specs=[
            pl.BlockSpec(memory_space=pltpu.SMEM),
            pl.BlockSpec((None, tm, d), lambda b, s: (b, s, 0)),
            pl.BlockSpec((None, N_MOD, d), lambda b, s: (b, 0, 0)),
            _layer_spec((1, d), layer),
            pl.BlockSpec((tm, LANES), lambda b, s: (s, 0)),
            pl.BlockSpec((tm, LANES), lambda b, s: (s, 0)),
            _layer_spec((CONV_K, d), layer),
            _const_spec(w_in.shape),
            _const_spec(wqk.shape),
            _const_spec(wcp.shape),
            _const_spec(wap.shape),
            _const_spec(wout.shape),
        ] + cin,
        out_specs=[pl.BlockSpec((None, tm, d), lambda b, s: (b, s, 0))] + cout,
        scratch_shapes=[
            pltpu.VMEM((SUBLANES + tm, d), F32),
            pltpu.VMEM((N_KV_HEADS, BLOCK + tm, D_KV), BF16),
            pltpu.VMEM((N_KV_HEADS, D_KV, BLOCK + tm), BF16),
            pltpu.VMEM((nblk, GROUP * BLOCK, D_KV), BF16),
            pltpu.VMEM((tm, D_ATTN), BF16),
        ],
        out_shape=[jax.ShapeDtypeStruct(x.shape, F32)] + cshape,
        compiler_params=pltpu.CompilerParams(
            dimension_semantics=("arbitrary", "arbitrary"),
            vmem_limit_bytes=VMEM_LIMIT_BYTES),
        name="mixer",
    )(sinks, x, mods, g, cos_t, sin_t, conv_w, w_in, wqk, wcp, wap, wout,
      *[cst.src for cst in casts])
    return outs[0], outs[1:]


def _pick_permuted_qk(w_rows):
    pieces = []
    for j in range(GROUP):
        for half in range(2):
            for g in range(N_KV_HEADS):
                start = OFF_Q + (g * GROUP + j) * HEAD_DIM + half * HALF
                pieces.append(w_rows[:, start:start + HALF])
    for half in range(2):
        for g in range(N_KV_HEADS):
            start = OFF_K + g * HEAD_DIM + half * HALF
            pieces.append(w_rows[:, start:start + HALF])
    return jnp.concatenate(pieces, axis=1)


def _qk_cast(w_in, layer, steps):
    rows, cols = w_in.shape[1:]
    idx = lambda t: (t, 0)
    return _Cast(w_in, layer, (rows // steps, cols), idx, idx, D_ATTN + D_KV, _pick_permuted_qk)


def _attn_proj_cast(w_attn_proj, layer, steps):
    rows, cols = w_attn_proj.shape[1:]
    blk = rows // steps
    per_head = HEAD_DIM // blk

    def in_index(t):
        chunk, within = t // per_head, t % per_head
        return ((chunk % N_KV_HEADS) * GROUP + chunk // N_KV_HEADS) * per_head + within, 0

    return _Cast(w_attn_proj, layer, (blk, cols), in_index, lambda t: (t, 0), cols,
                 lambda blk_rows: blk_rows)


def kernel(x, c, w_ada, b_ada, g_ffn1, w1_gu, w1_down, g_mix, w_in, conv_w,
           w_conv_proj, w_attn_proj, sinks, w_out, g_ffn2, w2_gu, w2_down, g_final):
    bsz, seq, d = x.shape
    depth = w_ada.shape[0]
    ffn_steps = bsz * (seq // TM_FFN)
    mix_steps = bsz * (seq // TM_MIX)

    inv = 1.0 / (ROPE_THETA ** (jnp.arange(0, HEAD_DIM, 2, dtype=F32) / HEAD_DIM))
    ang = jnp.arange(seq, dtype=F32)[:, None] * inv[None, :]
    cos_t = jnp.tile(jnp.cos(ang), (1, LANES // HALF))
    sin_t = jnp.tile(jnp.sin(ang), (1, LANES // HALF))

    c_pad = jnp.pad(c, ((0, SUBLANES - bsz % SUBLANES), (0, 0)))
    gf = g_final.reshape(1, d)
    b_ada3 = b_ada.reshape(depth, 1, -1)
    g1, gm, g2 = (t.reshape(depth, 1, d) for t in (g_ffn1, g_mix, g_ffn2))

    for l in range(depth):
        mods, (wgu1, wd1) = _ada(
            c_pad, w_ada, b_ada3, l,
            [_grid_cast(w1_gu, l, ADA_STEPS), _grid_cast(w1_down, l, ADA_STEPS)])
        mods = mods[:bsz].reshape(bsz, N_MOD, d)

        x, (win, wqk, wcp, wap, wout) = _ffn(
            x, mods, g1, l, wgu1, wd1, gf,
            [_grid_cast(w_in, l, ffn_steps), _qk_cast(w_in, l, ffn_steps),
             _grid_cast(w_conv_proj, l, ffn_steps), _attn_proj_cast(w_attn_proj, l, ffn_steps),
             _grid_cast(w_out, l, ffn_steps)],
            mod_base=0, final=False)

        x, (wgu2, wd2) = _mix(
            x, mods, gm, l, cos_t, sin_t, conv_w, win, wqk, wcp, wap, wout, sinks,
            [_grid_cast(w2_gu, l, mix_steps), _grid_cast(w2_down, l, mix_steps)])

        x, _ = _ffn(x, mods, g2, l, wgu2, wd2, gf, [], mod_base=6, final=(l == depth - 1))
    return x
```

```python
import functools
from typing import Callable, NamedTuple

import jax
import jax.numpy as jnp
from jax import lax
from jax.experimental import pallas as pl
from jax.experimental.pallas import tpu as pltpu

D_MODEL = 1024
CONV_K = 3
HEAD_DIM = 64
HALF = HEAD_DIM // 2
N_HEADS = 16
N_KV_HEADS = 4
GROUP = N_HEADS // N_KV_HEADS
D_ATTN = N_HEADS * HEAD_DIM
D_KV = N_KV_HEADS * HEAD_DIM
WINDOW = 128
BLOCK = 128
ROPE_THETA = 10000.0
D_FF = 2816
N_MOD = 9
EPS = 1e-6
NEG_INF = -1e30

LANES = 128
SUBLANES = 8
BF16_SUBLANES = 16
VMEM_LIMIT_BYTES = 56 * 1024 * 1024

TM_FFN = 1024
FFN_SUBTILES = 8
TM_MIX = 1024
MIX_SUB = 256
ADA_STEPS = 8

OFF_B, OFF_C, OFF_U = 0, D_MODEL, 2 * D_MODEL
OFF_Q = 3 * D_MODEL
OFF_K = OFF_Q + D_ATTN
OFF_V = OFF_K + D_KV
OFF_ZC = OFF_V + D_KV
OFF_ZA = OFF_ZC + D_MODEL

BF16 = jnp.bfloat16
F32 = jnp.float32


def _const_spec(shape):
    nd = len(shape)
    return pl.BlockSpec(shape, lambda *_: (0,) * nd, pipeline_mode=pl.Buffered(1))


def _layer_spec(shape, layer):
    nd = len(shape)
    return pl.BlockSpec((None,) + tuple(shape), lambda *_: (layer,) + (0,) * nd,
                        pipeline_mode=pl.Buffered(1))


class _Cast(NamedTuple):
    src: jax.Array
    layer: int
    block: tuple
    in_index: Callable
    out_index: Callable
    out_cols: int
    pick: Callable


def _grid_cast(src, layer, steps):
    rows, cols = src.shape[1:]
    col_blocks = 1
    while (rows * col_blocks) % (steps * BF16_SUBLANES):
        col_blocks *= 2
    row_blocks = steps // col_blocks
    idx = lambda t: (t // col_blocks, t % col_blocks)
    return _Cast(src, layer, (rows // row_blocks, cols // col_blocks), idx, idx, cols,
                 lambda blk: blk)


def _cast_specs(casts, linear_step):
    in_specs, out_specs, out_shapes = [], [], []
    for cst in casts:
        col_blocks = cst.src.shape[2] // cst.block[1]
        in_specs.append(pl.BlockSpec(
            (None,) + cst.block,
            lambda *idx, cst=cst: (cst.layer,) + cst.in_index(linear_step(*idx))))
        out_specs.append(pl.BlockSpec(
            (cst.block[0], cst.out_cols // col_blocks),
            lambda *idx, cst=cst: cst.out_index(linear_step(*idx))))
        out_shapes.append(jax.ShapeDtypeStruct((cst.src.shape[1], cst.out_cols), BF16))
    return in_specs, out_specs, out_shapes


def _run_casts(picks, src_refs, dst_refs):
    for pick, src, dst in zip(picks, src_refs, dst_refs):
        dst[...] = pick(src[...]).astype(BF16)


def _rms_mod(x, g, shift, scale):
    ms = jnp.mean(x * x, axis=-1, keepdims=True)
    y = x * lax.rsqrt(ms + EPS) * g
    return y * (1.0 + scale) + shift


def _ada_kernel(*refs, picks):
    n_cast = len(picks)
    c_ref, w_ref, b_ref = refs[:3]
    cast_in = refs[3:3 + n_cast]
    o_ref = refs[3 + n_cast]
    cast_out = refs[4 + n_cast:]
    c = c_ref[...]
    ca = (c * jax.nn.sigmoid(c)).astype(BF16)
    o_ref[...] = jnp.dot(ca, w_ref[...].astype(BF16),
                         preferred_element_type=F32) + b_ref[...]
    _run_casts(picks, cast_in, cast_out)


def _ada(c_pad, w_ada, b_ada, layer, casts):
    rows = c_pad.shape[0]
    n = w_ada.shape[2]
    tn = n // ADA_STEPS
    cin, cout, cshape = _cast_specs(casts, lambda j: j)
    outs = pl.pallas_call(
        functools.partial(_ada_kernel, picks=tuple(cst.pick for cst in casts)),
        grid=(ADA_STEPS,),
        in_specs=[
            pl.BlockSpec((rows, D_MODEL), lambda j: (0, 0)),
            pl.BlockSpec((None, D_MODEL, tn), lambda j: (layer, 0, j)),
            pl.BlockSpec((None, 1, tn), lambda j: (layer, 0, j)),
        ] + cin,
        out_specs=[pl.BlockSpec((rows, tn), lambda j: (0, j))] + cout,
        out_shape=[jax.ShapeDtypeStruct((rows, n), F32)] + cshape,
        compiler_params=pltpu.CompilerParams(
            dimension_semantics=("arbitrary",),
            vmem_limit_bytes=VMEM_LIMIT_BYTES),
        name="ada_mod",
    )(c_pad, w_ada, b_ada, *[cst.src for cst in casts])
    return outs[0], outs[1:]


def _ffn_kernel(*refs, mod_base, final, picks):
    n_cast = len(picks)
    x_ref, mod_ref, g_ref, wgu_ref, wd_ref, gf_ref = refs[:6]
    cast_in = refs[6:6 + n_cast]
    o_ref = refs[6 + n_cast]
    cast_out = refs[7 + n_cast:]
    shift = mod_ref[mod_base:mod_base + 1, :]
    scale = mod_ref[mod_base + 1:mod_base + 2, :]
    gate = mod_ref[mod_base + 2:mod_base + 3, :]
    rows = x_ref.shape[0] // FFN_SUBTILES
    for r in range(FFN_SUBTILES):
        x = x_ref[r * rows:(r + 1) * rows, :]
        h = _rms_mod(x, g_ref[...], shift, scale).astype(BF16)
        gu = jnp.dot(h, wgu_ref[...], preferred_element_type=F32)
        a = gu[:, :D_FF]
        b = gu[:, D_FF:]
        act = (a * jax.nn.sigmoid(a) * b).astype(BF16)
        f = jnp.dot(act, wd_ref[...], preferred_element_type=F32)
        xn = x + (0.5 * gate) * f
        if final:
            ms = jnp.mean(xn * xn, axis=-1, keepdims=True)
            xn = xn * lax.rsqrt(ms + EPS) * gf_ref[...]
        o_ref[r * rows:(r + 1) * rows, :] = xn
    _run_casts(picks, cast_in, cast_out)


def _ffn(x, mods, g, layer, wgu, wd, g_final, casts, *, mod_base, final):
    bsz, seq, d = x.shape
    n_s = seq // TM_FFN
    cin, cout, cshape = _cast_specs(casts, lambda b, s: b * n_s + s)
    kern = functools.partial(_ffn_kernel, mod_base=mod_base, final=final,
                             picks=tuple(cst.pick for cst in casts))
    outs = pl.pallas_call(
        kern,
        grid=(bsz, n_s),
        in_specs=[
            pl.BlockSpec((None, TM_FFN, d), lambda b, s: (b, s, 0)),
            pl.BlockSpec((None, N_MOD, d), lambda b, s: (b, 0, 0)),
            _layer_spec((1, d), layer),
            _const_spec(wgu.shape),
            _const_spec(wd.shape),
            _const_spec((1, d)),
        ] + cin,
        out_specs=[pl.BlockSpec((None, TM_FFN, d), lambda b, s: (b, s, 0))] + cout,
        out_shape=[jax.ShapeDtypeStruct(x.shape, F32)] + cshape,
        compiler_params=pltpu.CompilerParams(
            dimension_semantics=("arbitrary", "arbitrary"),
            vmem_limit_bytes=VMEM_LIMIT_BYTES),
        name="ffn_final" if final else "ffn",
    )(x, mods, g, wgu, wd, g_final, *[cst.src for cst in casts])
    return outs[0], outs[1:]


def _mix_kernel(*refs, layer, picks):
    n_cast = len(picks)
    (sinks_ref, x_ref, mod_ref, g_ref, cos_ref, sin_ref, convw_ref,
     win_ref, wqk_ref, wcp_ref, wap_ref, wout_ref) = refs[:12]
    cast_in = refs[12:12 + n_cast]
    o_ref = refs[12 + n_cast]
    cast_out = refs[13 + n_cast:13 + 2 * n_cast]
    cu_scr, k_scr, vt_scr = refs[13 + 2 * n_cast:]
    tm = x_ref.shape[0]
    nblk = tm // BLOCK
    s_idx = pl.program_id(1)

    @pl.when(s_idx == 0)
    def _():
        cu_scr[0:SUBLANES, :] = jnp.zeros((SUBLANES, D_MODEL), F32)
        k_scr[:, 0:BLOCK, :] = jnp.zeros((N_KV_HEADS, BLOCK, D_KV), BF16)
        vt_scr[:, :, 0:BLOCK] = jnp.zeros((N_KV_HEADS, D_KV, BLOCK), BF16)

    @pl.when(s_idx > 0)
    def _():
        cu_scr[0:SUBLANES, :] = cu_scr[tm:tm + SUBLANES, :]
        for g in range(N_KV_HEADS):
            k_scr[g, 0:BLOCK, :] = k_scr[g, tm:tm + BLOCK, :]
            vt_scr[g, :, 0:BLOCK] = vt_scr[g, :, tm:tm + BLOCK]

    shift = mod_ref[3:4, :]
    scale = mod_ref[4:5, :]
    gate = mod_ref[5:6, :]
    lane = lax.broadcasted_iota(jnp.int32, (1, D_KV), 1)
    head_of_lane = (lane % LANES) // HALF
    vrow = lax.broadcasted_iota(jnp.int32, (D_KV, 1), 0)
    head_of_vrow = vrow // HEAD_DIM
    qscale = HEAD_DIM ** -0.5

    key_slot = lax.broadcasted_iota(jnp.int32, (BLOCK, GROUP * BLOCK), 0)
    query = lax.broadcasted_iota(jnp.int32, (BLOCK, GROUP * BLOCK), 1)
    from_prev = key_slot > (query % BLOCK)
    neg_first = jnp.where(s_idx == 0, NEG_INF, 0.0)
    sink_rows = [
        jnp.concatenate([jnp.full((1, BLOCK), sinks_ref[layer, g * GROUP + j], F32)
                         for j in range(GROUP)], axis=1)
        for g in range(N_KV_HEADS)]

    def attend(q_all, lo, first):
        k_all = jnp.concatenate(
            [k_scr[g, lo:lo + 2 * BLOCK, :] for g in range(N_KV_HEADS)], axis=0)
        st_all = lax.dot_general(k_all, q_all, (((1,), (1,)), ((), ())),
                                 preferred_element_type=F32)
        pt_parts = []
        for g in range(N_KV_HEADS):
            s_prev = st_all[2 * g * BLOCK:(2 * g + 1) * BLOCK, :]
            s_cur = st_all[(2 * g + 1) * BLOCK:(2 * g + 2) * BLOCK, :]
            if first:
                s_prev = s_prev + neg_first
            s = jnp.where(from_prev, s_prev, s_cur)
            sink = sink_rows[g]
            m = jnp.maximum(jnp.max(s, axis=0, keepdims=True), sink)
            p = jnp.exp(s - m)
            den = jnp.sum(p, axis=0, keepdims=True) + jnp.exp(sink - m)
            pn = p * (1.0 / den)
            pt_parts.append(jnp.where(from_prev, pn, 0.0).astype(BF16))
            pt_parts.append(jnp.where(from_prev, 0.0, pn).astype(BF16))
        pt_all = jnp.concatenate(pt_parts, axis=0)
        vt_all = jnp.concatenate(
            [vt_scr[g, :, lo:lo + 2 * BLOCK] for g in range(N_KV_HEADS)], axis=1)
        o_all = jnp.dot(vt_all, pt_all, preferred_element_type=F32).T
        return jnp.concatenate([o_all[j * BLOCK:(j + 1) * BLOCK, :] for j in range(GROUP)],
                               axis=1).astype(BF16)

    sub = min(MIX_SUB, tm)
    for t in range(tm // sub):
        lo = t * sub
        x = x_ref[lo:lo + sub, :]
        h = _rms_mod(x, g_ref[...], shift, scale).astype(BF16)

        def proj(off, width):
            return jnp.dot(h, win_ref[:, off:off + width], preferred_element_type=F32)

        cu = proj(OFF_C, D_MODEL) * proj(OFF_U, D_MODEL)
        cu_scr[SUBLANES + lo:SUBLANES + lo + sub, :] = cu
        conv = (cu_scr[SUBLANES - 2 + lo:SUBLANES - 2 + lo + sub, :] * convw_ref[0:1, :]
                + cu_scr[SUBLANES - 1 + lo:SUBLANES - 1 + lo + sub, :] * convw_ref[1:2, :]
                + cu * convw_ref[2:3, :])
        bconv = (proj(OFF_B, D_MODEL) * conv).astype(BF16)
        y_conv = jnp.dot(bconv, wcp_ref[...], preferred_element_type=F32)

        cos = cos_ref[lo:lo + sub, :]
        sin = sin_ref[lo:lo + sub, :]

        k = jnp.dot(h, wqk_ref[:, D_ATTN:], preferred_element_type=F32)
        k1, k2 = k[:, :LANES], k[:, LANES:]
        kr = jnp.concatenate([k1 * cos - k2 * sin, k2 * cos + k1 * sin], axis=-1)
        vt = proj(OFF_V, D_KV).T
        for g in range(N_KV_HEADS):
            k_scr[g, BLOCK + lo:BLOCK + lo + sub, :] = (
                jnp.where(head_of_lane == g, kr, 0.0).astype(BF16))
            vt_scr[g, :, BLOCK + lo:BLOCK + lo + sub] = (
                jnp.where(head_of_vrow == g, vt, 0.0).astype(BF16))

        q = jnp.dot(h, wqk_ref[:, :D_ATTN], preferred_element_type=F32)
        q_rot = []
        for j in range(GROUP):
            q1 = q[:, j * D_KV:j * D_KV + LANES]
            q2 = q[:, j * D_KV + LANES:(j + 1) * D_KV]
            q_rot.append(jnp.concatenate([(q1 * cos - q2 * sin) * qscale,
                                          (q2 * cos + q1 * sin) * qscale],
                                         axis=-1).astype(BF16))
        attn_blocks = []
        for i in range(sub // BLOCK):
            q_all = jnp.concatenate([qr[i * BLOCK:(i + 1) * BLOCK, :] for qr in q_rot], axis=0)
            attn_blocks.append(attend(q_all, lo + i * BLOCK, first=(lo + i * BLOCK == 0)))
        attn = jnp.concatenate(attn_blocks, axis=0)
        y_attn = jnp.dot(attn, wap_ref[...], preferred_element_type=F32)

        merged = (jax.nn.sigmoid(proj(OFF_ZC, D_MODEL)) * y_conv
                  + jax.nn.sigmoid(proj(OFF_ZA, D_MODEL)) * y_attn).astype(BF16)
        o_ref[lo:lo + sub, :] = x + gate * jnp.dot(merged, wout_ref[...],
                                                   preferred_element_type=F32)
    _run_casts(picks, cast_in, cast_out)


def _mix(x, mods, g, layer, cos_t, sin_t, conv_w, w_in, wqk, wcp, wap, wout, sinks, casts):
    bsz, seq, d = x.shape
    tm = TM_MIX
    nblk = tm // BLOCK
    n_s = seq // tm
    cin, cout, cshape = _cast_specs(casts, lambda b, s: b * n_s + s)
    outs = pl.pallas_call(
        functools.partial(_mix_kernel, layer=layer, picks=tuple(cst.pick for cst in casts)),
        grid=(bsz, n_s),
        in_specs=[
            pl.BlockSpec(memory_space=pltpu.SMEM),
            pl.BlockSpec((None, tm, d), lambda b, s: (b, s, 0)),
            pl.BlockSpec((None, N_MOD, d), lambda b, s: (b, 0, 0)),
            _layer_spec((1, d), layer),
            pl.BlockSpec((tm, LANES), lambda b, s: (s, 0)),
            pl.BlockSpec((tm, LANES), lambda b, s: (s, 0)),
            _layer_spec((CONV_K, d), layer),
            _const_spec(w_in.shape),
            _const_spec(wqk.shape),
            _const_spec(wcp.shape),
            _const_spec(wap.shape),
            _const_spec(wout.shape),
        ] + cin,
        out_specs=[pl.BlockSpec((None, tm, d), lambda b, s: (b, s, 0))] + cout,
        scratch_shapes=[
            pltpu.VMEM((SUBLANES + tm, d), F32),
            pltpu.VMEM((N_KV_HEADS, BLOCK + tm, D_KV), BF16),
            pltpu.VMEM((N_KV_HEADS, D_KV, BLOCK + tm), BF16),
        ],
        out_shape=[jax.ShapeDtypeStruct(x.shape, F32)] + cshape,
        compiler_params=pltpu.CompilerParams(
            dimension_semantics=("arbitrary", "arbitrary"),
            vmem_limit_bytes=VMEM_LIMIT_BYTES),
        name="mixer",
    )(sinks, x, mods, g, cos_t, sin_t, conv_w, w_in, wqk, wcp, wap, wout,
      *[cst.src for cst in casts])
    return outs[0], outs[1:]


def _pick_permuted_qk(w_rows):
    pieces = []
    for j in range(GROUP):
        for half in range(2):
            for g in range(N_KV_HEADS):
                start = OFF_Q + (g * GROUP + j) * HEAD_DIM + half * HALF
                pieces.append(w_rows[:, start:start + HALF])
    for half in range(2):
        for g in range(N_KV_HEADS):
            start = OFF_K + g * HEAD_DIM + half * HALF
            pieces.append(w_rows[:, start:start + HALF])
    return jnp.concatenate(pieces, axis=1)


def _qk_cast(w_in, layer, steps):
    rows, cols = w_in.shape[1:]
    idx = lambda t: (t, 0)
    return _Cast(w_in, layer, (rows // steps, cols), idx, idx, D_ATTN + D_KV, _pick_permuted_qk)


def _attn_proj_cast(w_attn_proj, layer, steps):
    rows, cols = w_attn_proj.shape[1:]
    blk = rows // steps
    per_head = HEAD_DIM // blk

    def in_index(t):
        chunk, within = t // per_head, t % per_head
        return ((chunk % N_KV_HEADS) * GROUP + chunk // N_KV_HEADS) * per_head + within, 0

    return _Cast(w_attn_proj, layer, (blk, cols), in_index, lambda t: (t, 0), cols,
                 lambda blk_rows: blk_rows)


def kernel(x, c, w_ada, b_ada, g_ffn1, w1_gu, w1_down, g_mix, w_in, conv_w,
           w_conv_proj, w_attn_proj, sinks, w_out, g_ffn2, w2_gu, w2_down, g_final):
    bsz, seq, d = x.shape
    depth = w_ada.shape[0]
    ffn_steps = bsz * (seq // TM_FFN)
    mix_steps = bsz * (seq // TM_MIX)

    inv = 1.0 / (ROPE_THETA ** (jnp.arange(0, HEAD_DIM, 2, dtype=F32) / HEAD_DIM))
    ang = jnp.arange(seq, dtype=F32)[:, None] * inv[None, :]
    cos_t = jnp.tile(jnp.cos(ang), (1, LANES // HALF))
    sin_t = jnp.tile(jnp.sin(ang), (1, LANES // HALF))

    c_pad = jnp.pad(c, ((0, SUBLANES - bsz % SUBLANES), (0, 0)))
    gf = g_final.reshape(1, d)
    b_ada3 = b_ada.reshape(depth, 1, -1)
    g1, gm, g2 = (t.reshape(depth, 1, d) for t in (g_ffn1, g_mix, g_ffn2))

    for l in range(depth):
        mods, (wgu1, wd1) = _ada(
            c_pad, w_ada, b_ada3, l,
            [_grid_cast(w1_gu, l, ADA_STEPS), _grid_cast(w1_down, l, ADA_STEPS)])
        mods = mods[:bsz].reshape(bsz, N_MOD, d)

        x, (win, wqk, wcp, wap, wout) = _ffn(
            x, mods, g1, l, wgu1, wd1, gf,
            [_grid_cast(w_in, l, ffn_steps), _qk_cast(w_in, l, ffn_steps),
             _grid_cast(w_conv_proj, l, ffn_steps), _attn_proj_cast(w_attn_proj, l, ffn_steps),
             _grid_cast(w_out, l, ffn_steps)],
            mod_base=0, final=False)

        x, (wgu2, wd2) = _mix(
            x, mods, gm, l, cos_t, sin_t, conv_w, win, wqk, wcp, wap, wout, sinks,
            [_grid_cast(w2_gu, l, mix_steps), _grid_cast(w2_down, l, mix_steps)])

        x, _ = _ffn(x, mods, g2, l, wgu2, wd2, gf, [], mod_base=6, final=(l == depth - 1))
    return x
```

```python
import functools
from typing import Callable, NamedTuple

import jax
import jax.numpy as jnp
from jax import lax
from jax.experimental import pallas as pl
from jax.experimental.pallas import tpu as pltpu

D_MODEL = 1024
CONV_K = 3
HEAD_DIM = 64
HALF = HEAD_DIM // 2
N_HEADS = 16
N_KV_HEADS = 4
GROUP = N_HEADS // N_KV_HEADS
D_ATTN = N_HEADS * HEAD_DIM
D_KV = N_KV_HEADS * HEAD_DIM
WINDOW = 128
BLOCK = 128
QH = WINDOW // 2
ROPE_THETA = 10000.0
D_FF = 2816
N_MOD = 9
EPS = 1e-6
NEG_INF = -1e30

LANES = 128
SUBLANES = 8
BF16_SUBLANES = 16
VMEM_LIMIT_BYTES = 56 * 1024 * 1024

TM_FFN = 1024
FFN_SUBTILES = 8
TM_MIX = 1024
MIX_SUB = 256
ADA_STEPS = 8

OFF_B, OFF_C, OFF_U = 0, D_MODEL, 2 * D_MODEL
OFF_Q = 3 * D_MODEL
OFF_K = OFF_Q + D_ATTN
OFF_V = OFF_K + D_KV
OFF_ZC = OFF_V + D_KV
OFF_ZA = OFF_ZC + D_MODEL

BF16 = jnp.bfloat16
F32 = jnp.float32


def _const_spec(shape):
    nd = len(shape)
    return pl.BlockSpec(shape, lambda *_: (0,) * nd, pipeline_mode=pl.Buffered(1))


def _layer_spec(shape, layer):
    nd = len(shape)
    return pl.BlockSpec((None,) + tuple(shape), lambda *_: (layer,) + (0,) * nd,
                        pipeline_mode=pl.Buffered(1))


class _Cast(NamedTuple):
    src: jax.Array
    layer: int
    block: tuple
    in_index: Callable
    out_index: Callable
    out_cols: int
    pick: Callable


def _grid_cast(src, layer, steps):
    rows, cols = src.shape[1:]
    col_blocks = 1
    while (rows * col_blocks) % (steps * BF16_SUBLANES):
        col_blocks *= 2
    row_blocks = steps // col_blocks
    idx = lambda t: (t // col_blocks, t % col_blocks)
    return _Cast(src, layer, (rows // row_blocks, cols // col_blocks), idx, idx, cols,
                 lambda blk: blk)


def _cast_specs(casts, linear_step):
    in_specs, out_specs, out_shapes = [], [], []
    for cst in casts:
        col_blocks = cst.src.shape[2] // cst.block[1]
        in_specs.append(pl.BlockSpec(
            (None,) + cst.block,
            lambda *idx, cst=cst: (cst.layer,) + cst.in_index(linear_step(*idx))))
        out_specs.append(pl.BlockSpec(
            (cst.block[0], cst.out_cols // col_blocks),
            lambda *idx, cst=cst: cst.out_index(linear_step(*idx))))
        out_shapes.append(jax.ShapeDtypeStruct((cst.src.shape[1], cst.out_cols), BF16))
    return in_specs, out_specs, out_shapes


def _run_casts(picks, src_refs, dst_refs):
    for pick, src, dst in zip(picks, src_refs, dst_refs):
        dst[...] = pick(src[...]).astype(BF16)


def _rms_mod(x, g, shift, scale):
    ms = jnp.mean(x * x, axis=-1, keepdims=True)
    y = x * lax.rsqrt(ms + EPS) * g
    return y * (1.0 + scale) + shift


def _ada_kernel(*refs, picks):
    n_cast = len(picks)
    c_ref, w_ref, b_ref = refs[:3]
    cast_in = refs[3:3 + n_cast]
    o_ref = refs[3 + n_cast]
    cast_out = refs[4 + n_cast:]
    c = c_ref[...]
    ca = (c * jax.nn.sigmoid(c)).astype(BF16)
    o_ref[...] = jnp.dot(ca, w_ref[...].astype(BF16),
                         preferred_element_type=F32) + b_ref[...]
    _run_casts(picks, cast_in, cast_out)


def _ada(c_pad, w_ada, b_ada, layer, casts):
    rows = c_pad.shape[0]
    n = w_ada.shape[2]
    tn = n // ADA_STEPS
    cin, cout, cshape = _cast_specs(casts, lambda j: j)
    outs = pl.pallas_call(
        functools.partial(_ada_kernel, picks=tuple(cst.pick for cst in casts)),
        grid=(ADA_STEPS,),
        in_specs=[
            pl.BlockSpec((rows, D_MODEL), lambda j: (0, 0)),
            pl.BlockSpec((None, D_MODEL, tn), lambda j: (layer, 0, j)),
            pl.BlockSpec((None, 1, tn), lambda j: (layer, 0, j)),
        ] + cin,
        out_specs=[pl.BlockSpec((rows, tn), lambda j: (0, j))] + cout,
        out_shape=[jax.ShapeDtypeStruct((rows, n), F32)] + cshape,
        compiler_params=pltpu.CompilerParams(
            dimension_semantics=("arbitrary",),
            vmem_limit_bytes=VMEM_LIMIT_BYTES),
        name="ada_mod",
    )(c_pad, w_ada, b_ada, *[cst.src for cst in casts])
    return outs[0], outs[1:]


def _ffn_kernel(*refs, mod_base, final, picks):
    n_cast = len(picks)
    x_ref, mod_ref, g_ref, wgu_ref, wd_ref, gf_ref = refs[:6]
    cast_in = refs[6:6 + n_cast]
    o_ref = refs[6 + n_cast]
    cast_out = refs[7 + n_cast:]
    shift = mod_ref[mod_base:mod_base + 1, :]
    scale = mod_ref[mod_base + 1:mod_base + 2, :]
    gate = mod_ref[mod_base + 2:mod_base + 3, :]
    rows = x_ref.shape[0] // FFN_SUBTILES
    for r in range(FFN_SUBTILES):
        x = x_ref[r * rows:(r + 1) * rows, :]
        h = _rms_mod(x, g_ref[...], shift, scale).astype(BF16)
        gu = jnp.dot(h, wgu_ref[...], preferred_element_type=F32)
        a = gu[:, :D_FF]
        b = gu[:, D_FF:]
        act = (a * jax.nn.sigmoid(a) * b).astype(BF16)
        f = jnp.dot(act, wd_ref[...], preferred_element_type=F32)
        xn = x + (0.5 * gate) * f
        if final:
            ms = jnp.mean(xn * xn, axis=-1, keepdims=True)
            xn = xn * lax.rsqrt(ms + EPS) * gf_ref[...]
        o_ref[r * rows:(r + 1) * rows, :] = xn
    _run_casts(picks, cast_in, cast_out)


def _ffn(x, mods, g, layer, wgu, wd, g_final, casts, *, mod_base, final):
    bsz, seq, d = x.shape
    n_s = seq // TM_FFN
    cin, cout, cshape = _cast_specs(casts, lambda b, s: b * n_s + s)
    kern = functools.partial(_ffn_kernel, mod_base=mod_base, final=final,
                             picks=tuple(cst.pick for cst in casts))
    outs = pl.pallas_call(
        kern,
        grid=(bsz, n_s),
        in_specs=[
            pl.BlockSpec((None, TM_FFN, d), lambda b, s: (b, s, 0)),
            pl.BlockSpec((None, N_MOD, d), lambda b, s: (b, 0, 0)),
            _layer_spec((1, d), layer),
            _const_spec(wgu.shape),
            _const_spec(wd.shape),
            _const_spec((1, d)),
        ] + cin,
        out_specs=[pl.BlockSpec((None, TM_FFN, d), lambda b, s: (b, s, 0))] + cout,
        out_shape=[jax.ShapeDtypeStruct(x.shape, F32)] + cshape,
        compiler_params=pltpu.CompilerParams(
            dimension_semantics=("arbitrary", "arbitrary"),
            vmem_limit_bytes=VMEM_LIMIT_BYTES),
        name="ffn_final" if final else "ffn",
    )(x, mods, g, wgu, wd, g_final, *[cst.src for cst in casts])
    return outs[0], outs[1:]


def _mix_kernel(*refs, layer, picks):
    n_cast = len(picks)
    (sinks_ref, x_ref, mod_ref, g_ref, cos_ref, sin_ref, convw_ref,
     win_ref, wqk_ref, wcp_ref, wap_ref, wout_ref) = refs[:12]
    cast_in = refs[12:12 + n_cast]
    o_ref = refs[12 + n_cast]
    cast_out = refs[13 + n_cast:13 + 2 * n_cast]
    cu_scr, k_scr, vt_scr = refs[13 + 2 * n_cast:]
    tm = x_ref.shape[0]
    nblk = tm // BLOCK
    s_idx = pl.program_id(1)

    @pl.when(s_idx == 0)
    def _():
        cu_scr[0:SUBLANES, :] = jnp.zeros((SUBLANES, D_MODEL), F32)
        k_scr[:, 0:BLOCK, :] = jnp.zeros((N_KV_HEADS, BLOCK, D_KV), BF16)
        vt_scr[:, :, 0:BLOCK] = jnp.zeros((N_KV_HEADS, D_KV, BLOCK), BF16)

    @pl.when(s_idx > 0)
    def _():
        cu_scr[0:SUBLANES, :] = cu_scr[tm:tm + SUBLANES, :]
        for g in range(N_KV_HEADS):
            k_scr[g, 0:BLOCK, :] = k_scr[g, tm:tm + BLOCK, :]
            vt_scr[g, :, 0:BLOCK] = vt_scr[g, :, tm:tm + BLOCK]

    shift = mod_ref[3:4, :]
    scale = mod_ref[4:5, :]
    gate = mod_ref[5:6, :]
    lane = lax.broadcasted_iota(jnp.int32, (1, D_KV), 1)
    head_of_lane = (lane % LANES) // HALF
    vrow = lax.broadcasted_iota(jnp.int32, (D_KV, 1), 0)
    head_of_vrow = vrow // HEAD_DIM
    qscale = HEAD_DIM ** -0.5

    key_slot = lax.broadcasted_iota(jnp.int32, (QH, GROUP * QH), 0)
    query = lax.broadcasted_iota(jnp.int32, (QH, GROUP * QH), 1)
    from_a = key_slot > (query % QH)
    neg_first = jnp.where(s_idx == 0, NEG_INF, 0.0)
    sink_rows = [
        jnp.concatenate([jnp.full((1, QH), sinks_ref[layer, g * GROUP + j], F32)
                         for j in range(GROUP)], axis=1)
        for g in range(N_KV_HEADS)]

    def attend(q_all, klo):
        a_in_halo, b_in_halo = klo < BLOCK, klo + QH < BLOCK
        k_all = jnp.concatenate(
            [k_scr[g, klo:klo + 3 * QH, :] for g in range(N_KV_HEADS)], axis=0)
        st_all = lax.dot_general(k_all, q_all, (((1,), (1,)), ((), ())),
                                 preferred_element_type=F32)
        pt_parts = []
        for g in range(N_KV_HEADS):
            s_a = st_all[3 * g * QH:(3 * g + 1) * QH, :]
            s_b = st_all[(3 * g + 1) * QH:(3 * g + 2) * QH, :]
            s_c = st_all[(3 * g + 2) * QH:(3 * g + 3) * QH, :]
            if a_in_halo:
                s_a = s_a + neg_first
            if b_in_halo:
                s_b = s_b + neg_first
            s = jnp.concatenate([jnp.where(from_a, s_a, s_c), s_b], axis=0)
            sink = sink_rows[g]
            m = jnp.maximum(jnp.max(s, axis=0, keepdims=True), sink)
            p = jnp.exp(s - m)
            den = jnp.sum(p, axis=0, keepdims=True) + jnp.exp(sink - m)
            pn = p * (1.0 / den)
            pn_ac, pn_b = pn[:QH, :], pn[QH:, :]
            pt_parts.append(jnp.where(from_a, pn_ac, 0.0).astype(BF16))
            pt_parts.append(pn_b.astype(BF16))
            pt_parts.append(jnp.where(from_a, 0.0, pn_ac).astype(BF16))
        pt_all = jnp.concatenate(pt_parts, axis=0)
        vt_all = jnp.concatenate(
            [vt_scr[g, :, klo:klo + 3 * QH] for g in range(N_KV_HEADS)], axis=1)
        o_all = jnp.dot(vt_all, pt_all, preferred_element_type=F32).T
        return jnp.concatenate([o_all[j * QH:(j + 1) * QH, :] for j in range(GROUP)],
                               axis=1).astype(BF16)

    sub = min(MIX_SUB, tm)
    for t in range(tm // sub):
        lo = t * sub
        x = x_ref[lo:lo + sub, :]
        h = _rms_mod(x, g_ref[...], shift, scale).astype(BF16)

        def proj(off, width):
            return jnp.dot(h, win_ref[:, off:off + width], preferred_element_type=F32)

        cu = proj(OFF_C, D_MODEL) * proj(OFF_U, D_MODEL)
        cu_scr[SUBLANES + lo:SUBLANES + lo + sub, :] = cu
        conv = (cu_scr[SUBLANES - 2 + lo:SUBLANES - 2 + lo + sub, :] * convw_ref[0:1, :]
                + cu_scr[SUBLANES - 1 + lo:SUBLANES - 1 + lo + sub, :] * convw_ref[1:2, :]
                + cu * convw_ref[2:3, :])
        bconv = (proj(OFF_B, D_MODEL) * conv).astype(BF16)
        y_conv = jnp.dot(bconv, wcp_ref[...], preferred_element_type=F32)

        cos = cos_ref[lo:lo + sub, :]
        sin = sin_ref[lo:lo + sub, :]

        k = jnp.dot(h, wqk_ref[:, D_ATTN:], preferred_element_type=F32)
        k1, k2 = k[:, :LANES], k[:, LANES:]
        kr = jnp.concatenate([k1 * cos - k2 * sin, k2 * cos + k1 * sin], axis=-1)
        vt = proj(OFF_V, D_KV).T
        for g in range(N_KV_HEADS):
            k_scr[g, BLOCK + lo:BLOCK + lo + sub, :] = (
                jnp.where(head_of_lane == g, kr, 0.0).astype(BF16))
            vt_scr[g, :, BLOCK + lo:BLOCK + lo + sub] = (
                jnp.where(head_of_vrow == g, vt, 0.0).astype(BF16))

        q = jnp.dot(h, wqk_ref[:, :D_ATTN], preferred_element_type=F32)
        q_rot = []
        for j in range(GROUP):
            q1 = q[:, j * D_KV:j * D_KV + LANES]
            q2 = q[:, j * D_KV + LANES:(j + 1) * D_KV]
            q_rot.append(jnp.concatenate([(q1 * cos - q2 * sin) * qscale,
                                          (q2 * cos + q1 * sin) * qscale],
                                         axis=-1).astype(BF16))
        attn_blocks = []
        for i in range(sub // QH):
            q_all = jnp.concatenate([qr[i * QH:(i + 1) * QH, :] for qr in q_rot], axis=0)
            attn_blocks.append(attend(q_all, lo + i * QH))
        attn = jnp.concatenate(attn_blocks, axis=0)
        y_attn = jnp.dot(attn, wap_ref[...], preferred_element_type=F32)

        merged = (jax.nn.sigmoid(proj(OFF_ZC, D_MODEL)) * y_conv
                  + jax.nn.sigmoid(proj(OFF_ZA, D_MODEL)) * y_attn).astype(BF16)
        o_ref[lo:lo + sub, :] = x + gate * jnp.dot(merged, wout_ref[...],
                                                   preferred_element_type=F32)
    _run_casts(picks, cast_in, cast_out)


def _mix(x, mods, g, layer, cos_t, sin_t, conv_w, w_in, wqk, wcp, wap, wout, sinks, casts):
    bsz, seq, d = x.shape
    tm = TM_MIX
    nblk = tm // BLOCK
    n_s = seq // tm
    cin, cout, cshape = _cast_specs(casts, lambda b, s: b * n_s + s)
    outs = pl.pallas_call(
        functools.partial(_mix_kernel, layer=layer, picks=tuple(cst.pick for cst in casts)),
        grid=(bsz, n_s),
        in_specs=[
            pl.BlockSpec(memory_space=pltpu.SMEM),
            pl.BlockSpec((None, tm, d), lambda b, s: (b, s, 0)),
            pl.BlockSpec((None, N_MOD, d), lambda b, s: (b, 0, 0)),
            _layer_spec((1, d), layer),
            pl.BlockSpec((tm, LANES), lambda b, s: (s, 0)),
            pl.BlockSpec((tm, LANES), lambda b, s: (s, 0)),
            _layer_spec((CONV_K, d), layer),
            _const_spec(w_in.shape),
            _const_spec(wqk.shape),
            _const_spec(wcp.shape),
            _const_spec(wap.shape),
            _const_spec(wout.shape),
        ] + cin,
        out_specs=[pl.BlockSpec((None, tm, d), lambda b, s: (b, s, 0))] + cout,
        scratch_shapes=[
            pltpu.VMEM((SUBLANES + tm, d), F32),
            pltpu.VMEM((N_KV_HEADS, BLOCK + tm, D_KV), BF16),
            pltpu.VMEM((N_KV_HEADS, D_KV, BLOCK + tm), BF16),
        ],
        out_shape=[jax.ShapeDtypeStruct(x.shape, F32)] + cshape,
        compiler_params=pltpu.CompilerParams(
            dimension_semantics=("arbitrary", "arbitrary"),
            vmem_limit_bytes=VMEM_LIMIT_BYTES),
        name="mixer",
    )(sinks, x, mods, g, cos_t, sin_t, conv_w, w_in, wqk, wcp, wap, wout,
      *[cst.src for cst in casts])
    return outs[0], outs[1:]


def _pick_permuted_qk(w_rows):
    pieces = []
    for j in range(GROUP):
        for half in range(2):
            for g in range(N_KV_HEADS):
                start = OFF_Q + (g * GROUP + j) * HEAD_DIM + half * HALF
                pieces.append(w_rows[:, start:start + HALF])
    for half in range(2):
        for g in range(N_KV_HEADS):
            start = OFF_K + g * HEAD_DIM + half * HALF
            pieces.append(w_rows[:, start:start + HALF])
    return jnp.concatenate(pieces, axis=1)


def _qk_cast(w_in, layer, steps):
    rows, cols = w_in.shape[1:]
    idx = lambda t: (t, 0)
    return _Cast(w_in, layer, (rows // steps, cols), idx, idx, D_ATTN + D_KV, _pick_permuted_qk)


def _attn_proj_cast(w_attn_proj, layer, steps):
    rows, cols = w_attn_proj.shape[1:]
    blk = rows // steps
    per_head = HEAD_DIM // blk

    def in_index(t):
        chunk, within = t // per_head, t % per_head
        return ((chunk % N_KV_HEADS) * GROUP + chunk // N_KV_HEADS) * per_head + within, 0

    return _Cast(w_attn_proj, layer, (blk, cols), in_index, lambda t: (t, 0), cols,
                 lambda blk_rows: blk_rows)


def kernel(x, c, w_ada, b_ada, g_ffn1, w1_gu, w1_down, g_mix, w_in, conv_w,
           w_conv_proj, w_attn_proj, sinks, w_out, g_ffn2, w2_gu, w2_down, g_final):
    bsz, seq, d = x.shape
    depth = w_ada.shape[0]
    ffn_steps = bsz * (seq // TM_FFN)
    mix_steps = bsz * (seq // TM_MIX)

    inv = 1.0 / (ROPE_THETA ** (jnp.arange(0, HEAD_DIM, 2, dtype=F32) / HEAD_DIM))
    ang = jnp.arange(seq, dtype=F32)[:, None] * inv[None, :]
    cos_t = jnp.tile(jnp.cos(ang), (1, LANES // HALF))
    sin_t = jnp.tile(jnp.sin(ang), (1, LANES // HALF))

    c_pad = jnp.pad(c, ((0, SUBLANES - bsz % SUBLANES), (0, 0)))
    gf = g_final.reshape(1, d)
    b_ada3 = b_ada.reshape(depth, 1, -1)
    g1, gm, g2 = (t.reshape(depth, 1, d) for t in (g_ffn1, g_mix, g_ffn2))

    for l in range(depth):
        mods, (wgu1, wd1) = _ada(
            c_pad, w_ada, b_ada3, l,
            [_grid_cast(w1_gu, l, ADA_STEPS), _grid_cast(w1_down, l, ADA_STEPS)])
        mods = mods[:bsz].reshape(bsz, N_MOD, d)

        x, (win, wqk, wcp, wap, wout) = _ffn(
            x, mods, g1, l, wgu1, wd1, gf,
            [_grid_cast(w_in, l, ffn_steps), _qk_cast(w_in, l, ffn_steps),
             _grid_cast(w_conv_proj, l, ffn_steps), _attn_proj_cast(w_attn_proj, l, ffn_steps),
             _grid_cast(w_out, l, ffn_steps)],
            mod_base=0, final=False)

        x, (wgu2, wd2) = _mix(
            x, mods, gm, l, cos_t, sin_t, conv_w, win, wqk, wcp, wap, wout, sinks,
            [_grid_cast(w2_gu, l, mix_steps), _grid_cast(w2_down, l, mix_steps)])

        x, _ = _ffn(x, mods, g2, l, wgu2, wd2, gf, [], mod_base=6, final=(l == depth - 1))
    return x
```

```python
import functools
from typing import Callable, NamedTuple

import jax
import jax.numpy as jnp
from jax import lax
from jax.experimental import pallas as pl
from jax.experimental.pallas import tpu as pltpu

D_MODEL = 1024
CONV_K = 3
HEAD_DIM = 64
HALF = HEAD_DIM // 2
N_HEADS = 16
N_KV_HEADS = 4
GROUP = N_HEADS // N_KV_HEADS
D_ATTN = N_HEADS * HEAD_DIM
D_KV = N_KV_HEADS * HEAD_DIM
WINDOW = 128
BLOCK = 128
QH = WINDOW // 2
ROPE_THETA = 10000.0
D_FF = 2816
N_MOD = 9
EPS = 1e-6
NEG_INF = -1e30

LANES = 128
SUBLANES = 8
BF16_SUBLANES = 16
VMEM_BYTES = 64 * 1024 * 1024
VMEM_LIMIT_BYTES = VMEM_BYTES - 4 * 1024 * 1024

TM_FFN = 1024
FFN_SUBTILES = 8
TM_MIX = 1024
MIX_SUB = 256
ADA_STEPS = 8

OFF_B, OFF_C, OFF_U = 0, D_MODEL, 2 * D_MODEL
OFF_Q = 3 * D_MODEL
OFF_K = OFF_Q + D_ATTN
OFF_V = OFF_K + D_KV
OFF_ZC = OFF_V + D_KV
OFF_ZA = OFF_ZC + D_MODEL

BF16 = jnp.bfloat16
F32 = jnp.float32


def _const_spec(shape):
    nd = len(shape)
    return pl.BlockSpec(shape, lambda *_: (0,) * nd, pipeline_mode=pl.Buffered(1))


def _layer_spec(shape, layer):
    nd = len(shape)
    return pl.BlockSpec((None,) + tuple(shape), lambda *_: (layer,) + (0,) * nd,
                        pipeline_mode=pl.Buffered(1))


class _Cast(NamedTuple):
    src: jax.Array
    layer: int
    block: tuple
    in_index: Callable
    out_index: Callable
    out_cols: int
    pick: Callable


def _grid_cast(src, layer, steps):
    rows, cols = src.shape[1:]
    col_blocks = 1
    while (rows * col_blocks) % (steps * BF16_SUBLANES):
        col_blocks *= 2
    row_blocks = steps // col_blocks
    idx = lambda t: (t // col_blocks, t % col_blocks)
    return _Cast(src, layer, (rows // row_blocks, cols // col_blocks), idx, idx, cols,
                 lambda blk: blk)


def _cast_specs(casts, linear_step):
    in_specs, out_specs, out_shapes = [], [], []
    for cst in casts:
        col_blocks = cst.src.shape[2] // cst.block[1]
        in_specs.append(pl.BlockSpec(
            (None,) + cst.block,
            lambda *idx, cst=cst: (cst.layer,) + cst.in_index(linear_step(*idx))))
        out_specs.append(pl.BlockSpec(
            (cst.block[0], cst.out_cols // col_blocks),
            lambda *idx, cst=cst: cst.out_index(linear_step(*idx))))
        out_shapes.append(jax.ShapeDtypeStruct((cst.src.shape[1], cst.out_cols), BF16))
    return in_specs, out_specs, out_shapes


def _run_casts(picks, src_refs, dst_refs):
    for pick, src, dst in zip(picks, src_refs, dst_refs):
        dst[...] = pick(src[...]).astype(BF16)


def _rms_mod(x, g, shift, scale):
    ms = jnp.mean(x * x, axis=-1, keepdims=True)
    y = x * lax.rsqrt(ms + EPS) * g
    return y * (1.0 + scale) + shift


def _ada_kernel(*refs, picks):
    n_cast = len(picks)
    c_ref, w_ref, b_ref = refs[:3]
    cast_in = refs[3:3 + n_cast]
    o_ref = refs[3 + n_cast]
    cast_out = refs[4 + n_cast:]
    c = c_ref[...]
    ca = (c * jax.nn.sigmoid(c)).astype(BF16)
    o_ref[...] = jnp.dot(ca, w_ref[...].astype(BF16),
                         preferred_element_type=F32) + b_ref[...]
    _run_casts(picks, cast_in, cast_out)


def _ada(c_pad, w_ada, b_ada, layer, casts):
    rows = c_pad.shape[0]
    n = w_ada.shape[2]
    tn = n // ADA_STEPS
    cin, cout, cshape = _cast_specs(casts, lambda j: j)
    outs = pl.pallas_call(
        functools.partial(_ada_kernel, picks=tuple(cst.pick for cst in casts)),
        grid=(ADA_STEPS,),
        in_specs=[
            pl.BlockSpec((rows, D_MODEL), lambda j: (0, 0)),
            pl.BlockSpec((None, D_MODEL, tn), lambda j: (layer, 0, j)),
            pl.BlockSpec((None, 1, tn), lambda j: (layer, 0, j)),
        ] + cin,
        out_specs=[pl.BlockSpec((rows, tn), lambda j: (0, j))] + cout,
        out_shape=[jax.ShapeDtypeStruct((rows, n), F32)] + cshape,
        compiler_params=pltpu.CompilerParams(
            dimension_semantics=("arbitrary",),
            vmem_limit_bytes=VMEM_LIMIT_BYTES),
        name="ada_mod",
    )(c_pad, w_ada, b_ada, *[cst.src for cst in casts])
    return outs[0], outs[1:]


def _ffn_kernel(*refs, mod_base, final, picks):
    n_cast = len(picks)
    x_ref, mod_ref, g_ref, wgu_ref, wd_ref, gf_ref = refs[:6]
    cast_in = refs[6:6 + n_cast]
    o_ref = refs[6 + n_cast]
    cast_out = refs[7 + n_cast:]
    shift = mod_ref[mod_base:mod_base + 1, :]
    scale = mod_ref[mod_base + 1:mod_base + 2, :]
    gate = mod_ref[mod_base + 2:mod_base + 3, :]
    rows = x_ref.shape[0] // FFN_SUBTILES
    for r in range(FFN_SUBTILES):
        x = x_ref[r * rows:(r + 1) * rows, :]
        h = _rms_mod(x, g_ref[...], shift, scale).astype(BF16)
        gu = jnp.dot(h, wgu_ref[...], preferred_element_type=F32)
        a = gu[:, :D_FF]
        b = gu[:, D_FF:]
        act = (a * jax.nn.sigmoid(a) * b).astype(BF16)
        f = jnp.dot(act, wd_ref[...], preferred_element_type=F32)
        xn = x + (0.5 * gate) * f
        if final:
            ms = jnp.mean(xn * xn, axis=-1, keepdims=True)
            xn = xn * lax.rsqrt(ms + EPS) * gf_ref[...]
        o_ref[r * rows:(r + 1) * rows, :] = xn
    _run_casts(picks, cast_in, cast_out)


def _ffn(x, mods, g, layer, wgu, wd, g_final, casts, *, mod_base, final):
    bsz, seq, d = x.shape
    n_s = seq // TM_FFN
    cin, cout, cshape = _cast_specs(casts, lambda b, s: b * n_s + s)
    kern = functools.partial(_ffn_kernel, mod_base=mod_base, final=final,
                             picks=tuple(cst.pick for cst in casts))
    outs = pl.pallas_call(
        kern,
        grid=(bsz, n_s),
        in_specs=[
            pl.BlockSpec((None, TM_FFN, d), lambda b, s: (b, s, 0)),
            pl.BlockSpec((None, N_MOD, d), lambda b, s: (b, 0, 0)),
            _layer_spec((1, d), layer),
            _const_spec(wgu.shape),
            _const_spec(wd.shape),
            _const_spec((1, d)),
        ] + cin,
        out_specs=[pl.BlockSpec((None, TM_FFN, d), lambda b, s: (b, s, 0))] + cout,
        out_shape=[jax.ShapeDtypeStruct(x.shape, F32)] + cshape,
        compiler_params=pltpu.CompilerParams(
            dimension_semantics=("arbitrary", "arbitrary"),
            vmem_limit_bytes=VMEM_LIMIT_BYTES),
        name="ffn_final" if final else "ffn",
    )(x, mods, g, wgu, wd, g_final, *[cst.src for cst in casts])
    return outs[0], outs[1:]


def _mix_kernel(*refs, layer, picks):
    n_cast = len(picks)
    (sinks_ref, x_ref, mod_ref, g_ref, cos_ref, sin_ref, convw_ref,
     win_ref, wqk_ref, wcp_ref, wap_ref, wout_ref) = refs[:12]
    cast_in = refs[12:12 + n_cast]
    o_ref = refs[12 + n_cast]
    cast_out = refs[13 + n_cast:13 + 2 * n_cast]
    cu_scr, k_scr, vt_scr = refs[13 + 2 * n_cast:]
    tm = x_ref.shape[0]
    nblk = tm // BLOCK
    s_idx = pl.program_id(1)

    @pl.when(s_idx == 0)
    def _():
        cu_scr[0:SUBLANES, :] = jnp.zeros((SUBLANES, D_MODEL), F32)
        k_scr[:, 0:BLOCK, :] = jnp.zeros((N_KV_HEADS, BLOCK, D_KV), BF16)
        vt_scr[:, :, 0:BLOCK] = jnp.zeros((N_KV_HEADS, D_KV, BLOCK), BF16)

    @pl.when(s_idx > 0)
    def _():
        cu_scr[0:SUBLANES, :] = cu_scr[tm:tm + SUBLANES, :]
        for g in range(N_KV_HEADS):
            k_scr[g, 0:BLOCK, :] = k_scr[g, tm:tm + BLOCK, :]
            vt_scr[g, :, 0:BLOCK] = vt_scr[g, :, tm:tm + BLOCK]

    shift = mod_ref[3:4, :]
    scale = mod_ref[4:5, :]
    gate = mod_ref[5:6, :]
    lane = lax.broadcasted_iota(jnp.int32, (1, D_KV), 1)
    head_of_lane = (lane % LANES) // HALF
    vrow = lax.broadcasted_iota(jnp.int32, (D_KV, 1), 0)
    head_of_vrow = vrow // HEAD_DIM
    qscale = HEAD_DIM ** -0.5

    key_slot = lax.broadcasted_iota(jnp.int32, (QH, GROUP * QH), 0)
    query = lax.broadcasted_iota(jnp.int32, (QH, GROUP * QH), 1)
    from_a = key_slot > (query % QH)
    neg_first = jnp.where(s_idx == 0, NEG_INF, 0.0)
    sink_rows = [
        jnp.concatenate([jnp.full((1, QH), sinks_ref[layer, g * GROUP + j], F32)
                         for j in range(GROUP)], axis=1)
        for g in range(N_KV_HEADS)]

    def attend(q_all, klo):
        a_in_halo, b_in_halo = klo < BLOCK, klo + QH < BLOCK
        k_all = jnp.concatenate(
            [k_scr[g, klo:klo + 3 * QH, :] for g in range(N_KV_HEADS)], axis=0)
        st_all = lax.dot_general(k_all, q_all, (((1,), (1,)), ((), ())),
                                 preferred_element_type=F32)
        pt_parts = []
        for g in range(N_KV_HEADS):
            s_a = st_all[3 * g * QH:(3 * g + 1) * QH, :]
            s_b = st_all[(3 * g + 1) * QH:(3 * g + 2) * QH, :]
            s_c = st_all[(3 * g + 2) * QH:(3 * g + 3) * QH, :]
            if a_in_halo:
                s_a = s_a + neg_first
            if b_in_halo:
                s_b = s_b + neg_first
            s = jnp.concatenate([jnp.where(from_a, s_a, s_c), s_b], axis=0)
            sink = sink_rows[g]
            m = jnp.maximum(jnp.max(s, axis=0, keepdims=True), sink)
            p = jnp.exp(s - m)
            den = jnp.sum(p, axis=0, keepdims=True) + jnp.exp(sink - m)
            pn = p * (1.0 / den)
            pn_ac, pn_b = pn[:QH, :], pn[QH:, :]
            pt_parts.append(jnp.where(from_a, pn_ac, 0.0).astype(BF16))
            pt_parts.append(pn_b.astype(BF16))
            pt_parts.append(jnp.where(from_a, 0.0, pn_ac).astype(BF16))
        pt_all = jnp.concatenate(pt_parts, axis=0)
        vt_all = jnp.concatenate(
            [vt_scr[g, :, klo:klo + 3 * QH] for g in range(N_KV_HEADS)], axis=1)
        o_all = jnp.dot(vt_all, pt_all, preferred_element_type=F32).T
        return jnp.concatenate([o_all[j * QH:(j + 1) * QH, :] for j in range(GROUP)],
                               axis=1).astype(BF16)

    sub = min(MIX_SUB, tm)
    for t in range(tm // sub):
        lo = t * sub
        x = x_ref[lo:lo + sub, :]
        h = _rms_mod(x, g_ref[...], shift, scale).astype(BF16)

        def proj(off, width):
            return jnp.dot(h, win_ref[:, off:off + width], preferred_element_type=F32)

        cos = cos_ref[lo:lo + sub, :]
        sin = sin_ref[lo:lo + sub, :]

        k = jnp.dot(h, wqk_ref[:, D_ATTN:], preferred_element_type=F32)
        k1, k2 = k[:, :LANES], k[:, LANES:]
        kr = jnp.concatenate([k1 * cos - k2 * sin, k2 * cos + k1 * sin], axis=-1)
        vt = proj(OFF_V, D_KV).T
        for g in range(N_KV_HEADS):
            k_scr[g, BLOCK + lo:BLOCK + lo + sub, :] = (
                jnp.where(head_of_lane == g, kr, 0.0).astype(BF16))
            vt_scr[g, :, BLOCK + lo:BLOCK + lo + sub] = (
                jnp.where(head_of_vrow == g, vt, 0.0).astype(BF16))

        q = jnp.dot(h, wqk_ref[:, :D_ATTN], preferred_element_type=F32)
        q_rot = []
        for j in range(GROUP):
            q1 = q[:, j * D_KV:j * D_KV + LANES]
            q2 = q[:, j * D_KV + LANES:(j + 1) * D_KV]
            q_rot.append(jnp.concatenate([(q1 * cos - q2 * sin) * qscale,
                                          (q2 * cos + q1 * sin) * qscale],
                                         axis=-1).astype(BF16))
        other = [OFF_C, OFF_U, OFF_B, OFF_ZC, OFF_ZA]
        n_groups = sub // QH
        projs, attn_blocks = {}, []
        for i in range(n_groups):
            q_all = jnp.concatenate([qr[i * QH:(i + 1) * QH, :] for qr in q_rot], axis=0)
            attn_blocks.append(attend(q_all, lo + i * QH))
            for off in other[i::n_groups]:
                projs[off] = proj(off, D_MODEL)
        attn = jnp.concatenate(attn_blocks, axis=0)
        y_attn = jnp.dot(attn, wap_ref[...], preferred_element_type=F32)

        cu = projs[OFF_C] * projs[OFF_U]
        cu_scr[SUBLANES + lo:SUBLANES + lo + sub, :] = cu
        conv = (cu_scr[SUBLANES - 2 + lo:SUBLANES - 2 + lo + sub, :] * convw_ref[0:1, :]
                + cu_scr[SUBLANES - 1 + lo:SUBLANES - 1 + lo + sub, :] * convw_ref[1:2, :]
                + cu * convw_ref[2:3, :])
        bconv = (projs[OFF_B] * conv).astype(BF16)
        y_conv = jnp.dot(bconv, wcp_ref[...], preferred_element_type=F32)

        merged = (jax.nn.sigmoid(projs[OFF_ZC]) * y_conv
                  + jax.nn.sigmoid(projs[OFF_ZA]) * y_attn).astype(BF16)
        o_ref[lo:lo + sub, :] = x + gate * jnp.dot(merged, wout_ref[...],
                                                   preferred_element_type=F32)
    _run_casts(picks, cast_in, cast_out)


def _mix(x, mods, g, layer, cos_t, sin_t, conv_w, w_in, wqk, wcp, wap, wout, sinks, casts):
    bsz, seq, d = x.shape
    tm = TM_MIX
    nblk = tm // BLOCK
    n_s = seq // tm
    cin, cout, cshape = _cast_specs(casts, lambda b, s: b * n_s + s)
    outs = pl.pallas_call(
        functools.partial(_mix_kernel, layer=layer, picks=tuple(cst.pick for cst in casts)),
        grid=(bsz, n_s),
        in_specs=[
            pl.BlockSpec(memory_space=pltpu.SMEM),
            pl.BlockSpec((None, tm, d), lambda b, s: (b, s, 0)),
            pl.BlockSpec((None, N_MOD, d), lambda b, s: (b, 0, 0)),
            _layer_spec((1, d), layer),
            pl.BlockSpec((tm, LANES), lambda b, s: (s, 0)),
            pl.BlockSpec((tm, LANES), lambda b, s: (s, 0)),
            _layer_spec((CONV_K, d), layer),
            _const_spec(w_in.shape),
            _const_spec(wqk.shape),
            _const_spec(wcp.shape),
            _const_spec(wap.shape),
            _const_spec(wout.shape),
        ] + cin,
        out_specs=[pl.BlockSpec((None, tm, d), lambda b, s: (b, s, 0))] + cout,
        scratch_shapes=[
            pltpu.VMEM((SUBLANES + tm, d), F32),
            pltpu.VMEM((N_KV_HEADS, BLOCK + tm, D_KV), BF16),
            pltpu.VMEM((N_KV_HEADS, D_KV, BLOCK + tm), BF16),
        ],
        out_shape=[jax.ShapeDtypeStruct(x.shape, F32)] + cshape,
        compiler_params=pltpu.CompilerParams(
            dimension_semantics=("arbitrary", "arbitrary"),
            vmem_limit_bytes=VMEM_LIMIT_BYTES),
        name="mixer",
    )(sinks, x, mods, g, cos_t, sin_t, conv_w, w_in, wqk, wcp, wap, wout,
      *[cst.src for cst in casts])
    return outs[0], outs[1:]


def _pick_permuted_qk(w_rows):
    pieces = []
    for j in range(GROUP):
        for half in range(2):
            for g in range(N_KV_HEADS):
                start = OFF_Q + (g * GROUP + j) * HEAD_DIM + half * HALF
                pieces.append(w_rows[:, start:start + HALF])
    for half in range(2):
        for g in range(N_KV_HEADS):
            start = OFF_K + g * HEAD_DIM + half * HALF
            pieces.append(w_rows[:, start:start + HALF])
    return jnp.concatenate(pieces, axis=1)


def _qk_cast(w_in, layer, steps):
    rows, cols = w_in.shape[1:]
    idx = lambda t: (t, 0)
    return _Cast(w_in, layer, (rows // steps, cols), idx, idx, D_ATTN + D_KV, _pick_permuted_qk)


def _attn_proj_cast(w_attn_proj, layer, steps):
    rows, cols = w_attn_proj.shape[1:]
    blk = rows // steps
    per_head = HEAD_DIM // blk

    def in_index(t):
        chunk, within = t // per_head, t % per_head
        return ((chunk % N_KV_HEADS) * GROUP + chunk // N_KV_HEADS) * per_head + within, 0

    return _Cast(w_attn_proj, layer, (blk, cols), in_index, lambda t: (t, 0), cols,
                 lambda blk_rows: blk_rows)


def kernel(x, c, w_ada, b_ada, g_ffn1, w1_gu, w1_down, g_mix, w_in, conv_w,
           w_conv_proj, w_attn_proj, sinks, w_out, g_ffn2, w2_gu, w2_down, g_final):
    bsz, seq, d = x.shape
    depth = w_ada.shape[0]
    ffn_steps = bsz * (seq // TM_FFN)
    mix_steps = bsz * (seq // TM_MIX)

    inv = 1.0 / (ROPE_THETA ** (jnp.arange(0, HEAD_DIM, 2, dtype=F32) / HEAD_DIM))
    ang = jnp.arange(seq, dtype=F32)[:, None] * inv[None, :]
    cos_t = jnp.tile(jnp.cos(ang), (1, LANES // HALF))
    sin_t = jnp.tile(jnp.sin(ang), (1, LANES // HALF))

    c_pad = jnp.pad(c, ((0, SUBLANES - bsz % SUBLANES), (0, 0)))
    gf = g_final.reshape(1, d)
    b_ada3 = b_ada.reshape(depth, 1, -1)
    g1, gm, g2 = (t.reshape(depth, 1, d) for t in (g_ffn1, g_mix, g_ffn2))

    for l in range(depth):
        mods, (wgu1, wd1) = _ada(
            c_pad, w_ada, b_ada3, l,
            [_grid_cast(w1_gu, l, ADA_STEPS), _grid_cast(w1_down, l, ADA_STEPS)])
        mods = mods[:bsz].reshape(bsz, N_MOD, d)

        x, (win, wqk, wcp, wap, wout) = _ffn(
            x, mods, g1, l, wgu1, wd1, gf,
            [_grid_cast(w_in, l, ffn_steps), _qk_cast(w_in, l, ffn_steps),
             _grid_cast(w_conv_proj, l, ffn_steps), _attn_proj_cast(w_attn_proj, l, ffn_steps),
             _grid_cast(w_out, l, ffn_steps)],
            mod_base=0, final=False)

        x, (wgu2, wd2) = _mix(
            x, mods, gm, l, cos_t, sin_t, conv_w, win, wqk, wcp, wap, wout, sinks,
            [_grid_cast(w2_gu, l, mix_steps), _grid_cast(w2_down, l, mix_steps)])

        x, _ = _ffn(x, mods, g2, l, wgu2, wd2, gf, [], mod_base=6, final=(l == depth - 1))
    return x
```

```python
import functools
from typing import Callable, NamedTuple

import jax
import jax.numpy as jnp
from jax import lax
from jax.experimental import pallas as pl
from jax.experimental.pallas import tpu as pltpu

D_MODEL = 1024
CONV_K = 3
HEAD_DIM = 64
HALF = HEAD_DIM // 2
N_HEADS = 16
N_KV_HEADS = 4
GROUP = N_HEADS // N_KV_HEADS
D_ATTN = N_HEADS * HEAD_DIM
D_KV = N_KV_HEADS * HEAD_DIM
WINDOW = 128
BLOCK = 128
QH = WINDOW // 2
ROPE_THETA = 10000.0
D_FF = 2816
N_MOD = 9
EPS = 1e-6
NEG_INF = -1e30

LANES = 128
SUBLANES = 8
BF16_SUBLANES = 16
VMEM_BYTES = 64 * 1024 * 1024
VMEM_LIMIT_BYTES = VMEM_BYTES - 4 * 1024 * 1024

TM_FFN = 1024
FFN_SUBTILES = 8
TM_MIX = 1024
MIX_SUB = 512
ADA_STEPS = 8

OFF_B, OFF_C, OFF_U = 0, D_MODEL, 2 * D_MODEL
OFF_Q = 3 * D_MODEL
OFF_K = OFF_Q + D_ATTN
OFF_V = OFF_K + D_KV
OFF_ZC = OFF_V + D_KV
OFF_ZA = OFF_ZC + D_MODEL
D_IN = OFF_ZA + D_MODEL
D_QK = D_ATTN + D_KV


def _compact(off):
    return off if off < OFF_Q else off - D_QK

BF16 = jnp.bfloat16
F32 = jnp.float32


def _const_spec(shape):
    nd = len(shape)
    return pl.BlockSpec(shape, lambda *_: (0,) * nd, pipeline_mode=pl.Buffered(1))


def _layer_spec(shape, layer):
    nd = len(shape)
    return pl.BlockSpec((None,) + tuple(shape), lambda *_: (layer,) + (0,) * nd,
                        pipeline_mode=pl.Buffered(1))


class _Cast(NamedTuple):
    src: jax.Array
    layer: int
    block: tuple
    in_index: Callable
    out_index: Callable
    out_cols: int
    pick: Callable


def _grid_cast(src, layer, steps):
    rows, cols = src.shape[1:]
    col_blocks = 1
    while (rows * col_blocks) % (steps * BF16_SUBLANES):
        col_blocks *= 2
    row_blocks = steps // col_blocks
    idx = lambda t: (t // col_blocks, t % col_blocks)
    return _Cast(src, layer, (rows // row_blocks, cols // col_blocks), idx, idx, cols,
                 lambda blk: blk)


def _cast_specs(casts, linear_step):
    in_specs, out_specs, out_shapes = [], [], []
    for cst in casts:
        col_blocks = cst.src.shape[2] // cst.block[1]
        in_specs.append(pl.BlockSpec(
            (None,) + cst.block,
            lambda *idx, cst=cst: (cst.layer,) + cst.in_index(linear_step(*idx))))
        out_specs.append(pl.BlockSpec(
            (cst.block[0], cst.out_cols // col_blocks),
            lambda *idx, cst=cst: cst.out_index(linear_step(*idx))))
        out_shapes.append(jax.ShapeDtypeStruct((cst.src.shape[1], cst.out_cols), BF16))
    return in_specs, out_specs, out_shapes


def _run_casts(picks, src_refs, dst_refs):
    for pick, src, dst in zip(picks, src_refs, dst_refs):
        dst[...] = pick(src[...]).astype(BF16)


def _rms_mod(x, g, shift, scale):
    ms = jnp.mean(x * x, axis=-1, keepdims=True)
    y = x * lax.rsqrt(ms + EPS) * g
    return y * (1.0 + scale) + shift


def _ada_kernel(*refs, picks):
    n_cast = len(picks)
    c_ref, w_ref, b_ref = refs[:3]
    cast_in = refs[3:3 + n_cast]
    o_ref = refs[3 + n_cast]
    cast_out = refs[4 + n_cast:]
    c = c_ref[...]
    ca = (c * jax.nn.sigmoid(c)).astype(BF16)
    o_ref[...] = jnp.dot(ca, w_ref[...].astype(BF16),
                         preferred_element_type=F32) + b_ref[...]
    _run_casts(picks, cast_in, cast_out)


def _ada(c_pad, w_ada, b_ada, layer, casts):
    rows = c_pad.shape[0]
    n = w_ada.shape[2]
    tn = n // ADA_STEPS
    cin, cout, cshape = _cast_specs(casts, lambda j: j)
    outs = pl.pallas_call(
        functools.partial(_ada_kernel, picks=tuple(cst.pick for cst in casts)),
        grid=(ADA_STEPS,),
        in_specs=[
            pl.BlockSpec((rows, D_MODEL), lambda j: (0, 0)),
            pl.BlockSpec((None, D_MODEL, tn), lambda j: (layer, 0, j)),
            pl.BlockSpec((None, 1, tn), lambda j: (layer, 0, j)),
        ] + cin,
        out_specs=[pl.BlockSpec((rows, tn), lambda j: (0, j))] + cout,
        out_shape=[jax.ShapeDtypeStruct((rows, n), F32)] + cshape,
        compiler_params=pltpu.CompilerParams(
            dimension_semantics=("arbitrary",),
            vmem_limit_bytes=VMEM_LIMIT_BYTES),
        name="ada_mod",
    )(c_pad, w_ada, b_ada, *[cst.src for cst in casts])
    return outs[0], outs[1:]


def _ffn_kernel(*refs, mod_base, final, picks):
    n_cast = len(picks)
    x_ref, mod_ref, g_ref, wgu_ref, wd_ref, gf_ref = refs[:6]
    cast_in = refs[6:6 + n_cast]
    o_ref = refs[6 + n_cast]
    cast_out = refs[7 + n_cast:]
    shift = mod_ref[mod_base:mod_base + 1, :]
    scale = mod_ref[mod_base + 1:mod_base + 2, :]
    gate = mod_ref[mod_base + 2:mod_base + 3, :]
    rows = x_ref.shape[0] // FFN_SUBTILES
    for r in range(FFN_SUBTILES):
        x = x_ref[r * rows:(r + 1) * rows, :]
        h = _rms_mod(x, g_ref[...], shift, scale).astype(BF16)
        gu = jnp.dot(h, wgu_ref[...], preferred_element_type=F32)
        a = gu[:, :D_FF]
        b = gu[:, D_FF:]
        act = (a * jax.nn.sigmoid(a) * b).astype(BF16)
        f = jnp.dot(act, wd_ref[...], preferred_element_type=F32)
        xn = x + (0.5 * gate) * f
        if final:
            ms = jnp.mean(xn * xn, axis=-1, keepdims=True)
            xn = xn * lax.rsqrt(ms + EPS) * gf_ref[...]
        o_ref[r * rows:(r + 1) * rows, :] = xn
    _run_casts(picks, cast_in, cast_out)


def _ffn(x, mods, g, layer, wgu, wd, g_final, casts, *, mod_base, final):
    bsz, seq, d = x.shape
    n_s = seq // TM_FFN
    cin, cout, cshape = _cast_specs(casts, lambda b, s: b * n_s + s)
    kern = functools.partial(_ffn_kernel, mod_base=mod_base, final=final,
                             picks=tuple(cst.pick for cst in casts))
    outs = pl.pallas_call(
        kern,
        grid=(bsz, n_s),
        in_specs=[
            pl.BlockSpec((None, TM_FFN, d), lambda b, s: (b, s, 0)),
            pl.BlockSpec((None, N_MOD, d), lambda b, s: (b, 0, 0)),
            _layer_spec((1, d), layer),
            _const_spec(wgu.shape),
            _const_spec(wd.shape),
            _const_spec((1, d)),
        ] + cin,
        out_specs=[pl.BlockSpec((None, TM_FFN, d), lambda b, s: (b, s, 0))] + cout,
        out_shape=[jax.ShapeDtypeStruct(x.shape, F32)] + cshape,
        compiler_params=pltpu.CompilerParams(
            dimension_semantics=("arbitrary", "arbitrary"),
            vmem_limit_bytes=VMEM_LIMIT_BYTES),
        name="ffn_final" if final else "ffn",
    )(x, mods, g, wgu, wd, g_final, *[cst.src for cst in casts])
    return outs[0], outs[1:]


def _mix_kernel(*refs, layer, picks):
    n_cast = len(picks)
    (sinks_ref, x_ref, mod_ref, g_ref, cos_ref, sin_ref, convw_ref,
     win_ref, wqk_ref, wcp_ref, wap_ref, wout_ref) = refs[:12]
    cast_in = refs[12:12 + n_cast]
    o_ref = refs[12 + n_cast]
    cast_out = refs[13 + n_cast:13 + 2 * n_cast]
    cu_scr, k_scr, vt_scr = refs[13 + 2 * n_cast:]
    tm = x_ref.shape[0]
    nblk = tm // BLOCK
    s_idx = pl.program_id(1)

    @pl.when(s_idx == 0)
    def _():
        cu_scr[0:SUBLANES, :] = jnp.zeros((SUBLANES, D_MODEL), F32)
        k_scr[:, 0:BLOCK, :] = jnp.zeros((N_KV_HEADS, BLOCK, D_KV), BF16)
        vt_scr[:, :, 0:BLOCK] = jnp.zeros((N_KV_HEADS, D_KV, BLOCK), BF16)

    @pl.when(s_idx > 0)
    def _():
        cu_scr[0:SUBLANES, :] = cu_scr[tm:tm + SUBLANES, :]
        for g in range(N_KV_HEADS):
            k_scr[g, 0:BLOCK, :] = k_scr[g, tm:tm + BLOCK, :]
            vt_scr[g, :, 0:BLOCK] = vt_scr[g, :, tm:tm + BLOCK]

    shift = mod_ref[3:4, :]
    scale = mod_ref[4:5, :]
    gate = mod_ref[5:6, :]
    lane = lax.broadcasted_iota(jnp.int32, (1, D_KV), 1)
    head_of_lane = (lane % LANES) // HALF
    vrow = lax.broadcasted_iota(jnp.int32, (D_KV, 1), 0)
    head_of_vrow = vrow // HEAD_DIM
    qscale = HEAD_DIM ** -0.5

    key_slot = lax.broadcasted_iota(jnp.int32, (QH, GROUP * QH), 0)
    query = lax.broadcasted_iota(jnp.int32, (QH, GROUP * QH), 1)
    from_a = key_slot > (query % QH)
    neg_first = jnp.where(s_idx == 0, NEG_INF, 0.0)
    sink_rows = [
        jnp.concatenate([jnp.full((1, QH), sinks_ref[layer, g * GROUP + j], F32)
                         for j in range(GROUP)], axis=1)
        for g in range(N_KV_HEADS)]

    def attend(q_all, klo):
        a_in_halo, b_in_halo = klo < BLOCK, klo + QH < BLOCK
        k_all = jnp.concatenate(
            [k_scr[g, klo:klo + 3 * QH, :] for g in range(N_KV_HEADS)], axis=0)
        st_all = lax.dot_general(k_all, q_all, (((1,), (1,)), ((), ())),
                                 preferred_element_type=F32)
        pt_parts, inv_den = [], []
        for g in range(N_KV_HEADS):
            s_a =st_all[3 * g * QH:(3 * g + 1) * QH, :]
            s_b = st_all[(3 * g + 1) * QH:(3 * g + 2) * QH, :]
            s_c = st_all[(3 * g + 2) * QH:(3 * g + 3) * QH, :]
            if a_in_halo:
                s_a = s_a + neg_first
            if b_in_halo:
                s_b = s_b + neg_first
            s = jnp.concatenate([jnp.where(from_a, s_a, s_c), s_b], axis=0)
            sink = sink_rows[g]
            m = jnp.maximum(jnp.max(s, axis=0, keepdims=True), sink)
            p = jnp.exp(s - m)
            den = jnp.sum(p, axis=0, keepdims=True) + jnp.exp(sink - m)
            inv_den.append(jnp.broadcast_to(1.0 / den, (HEAD_DIM, GROUP * QH)))
            p_ac, p_b = p[:QH, :], p[QH:, :]
            pt_parts.append(jnp.where(from_a, p_ac, 0.0).astype(BF16))
            pt_parts.append(p_b.astype(BF16))
            pt_parts.append(jnp.where(from_a, 0.0, p_ac).astype(BF16))
        pt_all = jnp.concatenate(pt_parts, axis=0)
        vt_all = jnp.concatenate(
            [vt_scr[g, :, klo:klo + 3 * QH] for g in range(N_KV_HEADS)], axis=1)
        ot_all = jnp.dot(vt_all, pt_all, preferred_element_type=F32)
        o_all = (ot_all * jnp.concatenate(inv_den, axis=0)).T
        return jnp.concatenate([o_all[j * QH:(j + 1) * QH, :] for j in range(GROUP)],
                               axis=1).astype(BF16)

    sub = min(MIX_SUB, tm)
    for t in range(tm // sub):
        lo = t * sub
        x = x_ref[lo:lo + sub, :]
        h = _rms_mod(x, g_ref[...], shift, scale).astype(BF16)

        def proj(off, width):
            lo_col = _compact(off)
            return jnp.dot(h, win_ref[:, lo_col:lo_col + width], preferred_element_type=F32)

        cos = cos_ref[lo:lo + sub, :]
        sin = sin_ref[lo:lo + sub, :]

        k = jnp.dot(h, wqk_ref[:, D_ATTN:], preferred_element_type=F32)
        k1, k2 = k[:, :LANES], k[:, LANES:]
        kr = jnp.concatenate([k1 * cos - k2 * sin, k2 * cos + k1 * sin], axis=-1)
        vt = proj(OFF_V, D_KV).T
        for g in range(N_KV_HEADS):
            k_scr[g, BLOCK + lo:BLOCK + lo + sub, :] = (
                jnp.where(head_of_lane == g, kr, 0.0).astype(BF16))
            vt_scr[g, :, BLOCK + lo:BLOCK + lo + sub] = (
                jnp.where(head_of_vrow == g, vt, 0.0).astype(BF16))

        q = jnp.dot(h, wqk_ref[:, :D_ATTN], preferred_element_type=F32)
        q_rot = []
        for j in range(GROUP):
            q1 = q[:, j * D_KV:j * D_KV + LANES]
            q2 = q[:, j * D_KV + LANES:(j + 1) * D_KV]
            q_rot.append(jnp.concatenate([(q1 * cos - q2 * sin) * qscale,
                                          (q2 * cos + q1 * sin) * qscale],
                                         axis=-1).astype(BF16))
        other = [OFF_C, OFF_U, OFF_B, OFF_ZC, OFF_ZA]
        n_groups = sub // QH
        projs, attn_blocks = {}, []
        for i in range(n_groups):
            q_all = jnp.concatenate([qr[i * QH:(i + 1) * QH, :] for qr in q_rot], axis=0)
            attn_blocks.append(attend(q_all, lo + i * QH))
            for off in other[i::n_groups]:
                projs[off] = proj(off, D_MODEL)
        attn = jnp.concatenate(attn_blocks, axis=0)
        y_attn = jnp.dot(attn, wap_ref[...], preferred_element_type=F32)

        cu = projs[OFF_C] * projs[OFF_U]
        cu_scr[SUBLANES + lo:SUBLANES + lo + sub, :] = cu
        conv = (cu_scr[SUBLANES - 2 + lo:SUBLANES - 2 + lo + sub, :] * convw_ref[0:1, :]
                + cu_scr[SUBLANES - 1 + lo:SUBLANES - 1 + lo + sub, :] * convw_ref[1:2, :]
                + cu * convw_ref[2:3, :])
        bconv = (projs[OFF_B] * conv).astype(BF16)
        y_conv = jnp.dot(bconv, wcp_ref[...], preferred_element_type=F32)

        merged = (jax.nn.sigmoid(projs[OFF_ZC]) * y_conv
                  + jax.nn.sigmoid(projs[OFF_ZA]) * y_attn).astype(BF16)
        o_ref[lo:lo + sub, :] = x + gate * jnp.dot(merged, wout_ref[...],
                                                   preferred_element_type=F32)
    _run_casts(picks, cast_in, cast_out)


def _mix(x, mods, g, layer, cos_t, sin_t, conv_w, w_in, wqk, wcp, wap, wout, sinks, casts):
    bsz, seq, d = x.shape
    tm = TM_MIX
    nblk = tm // BLOCK
    n_s = seq // tm
    cin, cout, cshape = _cast_specs(casts, lambda b, s: b * n_s + s)
    outs = pl.pallas_call(
        functools.partial(_mix_kernel, layer=layer, picks=tuple(cst.pick for cst in casts)),
        grid=(bsz, n_s),
        in_specs=[
            pl.BlockSpec(memory_space=pltpu.SMEM),
            pl.BlockSpec((None, tm, d), lambda b, s: (b, s, 0)),
            pl.BlockSpec((None, N_MOD, d), lambda b, s: (b, 0, 0)),
            _layer_spec((1, d), layer),
            pl.BlockSpec((tm, LANES), lambda b, s: (s, 0)),
            pl.BlockSpec((tm, LANES), lambda b, s: (s, 0)),
            _layer_spec((CONV_K, d), layer),
            _const_spec(w_in.shape),
            _const_spec(wqk.shape),
            _const_spec(wcp.shape),
            _const_spec(wap.shape),
            _const_spec(wout.shape),
        ] + cin,
        out_specs=[pl.BlockSpec((None, tm, d), lambda b, s: (b, s, 0))] + cout,
        scratch_shapes=[
            pltpu.VMEM((SUBLANES + tm, d), F32),
            pltpu.VMEM((N_KV_HEADS, BLOCK + tm, D_KV), BF16),
            pltpu.VMEM((N_KV_HEADS, D_KV, BLOCK + tm), BF16),
        ],
        out_shape=[jax.ShapeDtypeStruct(x.shape, F32)] + cshape,
        compiler_params=pltpu.CompilerParams(
            dimension_semantics=("arbitrary", "arbitrary"),
            vmem_limit_bytes=VMEM_LIMIT_BYTES),
        name="mixer",
    )(sinks, x, mods, g, cos_t, sin_t, conv_w, w_in, wqk, wcp, wap, wout,
      *[cst.src for cst in casts])
    return outs[0], outs[1:]


def _pick_permuted_qk(w_rows):
    pieces = []
    for j in range(GROUP):
        for half in range(2):
            for g in range(N_KV_HEADS):
                start = OFF_Q + (g * GROUP + j) * HEAD_DIM + half * HALF
                pieces.append(w_rows[:, start:start + HALF])
    for half in range(2):
        for g in range(N_KV_HEADS):
            start = OFF_K + g * HEAD_DIM + half * HALF
            pieces.append(w_rows[:, start:start + HALF])
    return jnp.concatenate(pieces, axis=1)


def _qk_cast(w_in, layer, steps):
    rows, cols = w_in.shape[1:]
    idx = lambda t: (t, 0)
    return _Cast(w_in, layer, (rows // steps, cols), idx, idx, D_QK, _pick_permuted_qk)


def _non_qk_cast(w_in, layer, steps):
    rows, cols = w_in.shape[1:]
    idx = lambda t: (t, 0)
    pick = lambda w_rows: jnp.concatenate([w_rows[:, :OFF_Q], w_rows[:, OFF_V:]], axis=1)
    return _Cast(w_in, layer, (rows // steps, cols), idx, idx, D_IN - D_QK, pick)


def _attn_proj_cast(w_attn_proj, layer, steps):
    rows, cols = w_attn_proj.shape[1:]
    blk = rows // steps
    per_head = HEAD_DIM // blk

    def in_index(t):
        chunk, within = t // per_head, t % per_head
        return ((chunk % N_KV_HEADS) * GROUP + chunk // N_KV_HEADS) * per_head + within, 0

    return _Cast(w_attn_proj, layer, (blk, cols), in_index, lambda t: (t, 0), cols,
                 lambda blk_rows: blk_rows)


def kernel(x, c, w_ada, b_ada, g_ffn1, w1_gu, w1_down, g_mix, w_in, conv_w,
           w_conv_proj, w_attn_proj, sinks, w_out, g_ffn2, w2_gu, w2_down, g_final):
    bsz, seq, d = x.shape
    depth = w_ada.shape[0]
    ffn_steps = bsz * (seq // TM_FFN)
    mix_steps = bsz * (seq // TM_MIX)

    inv = 1.0 / (ROPE_THETA ** (jnp.arange(0, HEAD_DIM, 2, dtype=F32) / HEAD_DIM))
    ang = jnp.arange(seq, dtype=F32)[:, None] * inv[None, :]
    cos_t = jnp.tile(jnp.cos(ang), (1, LANES // HALF))
    sin_t = jnp.tile(jnp.sin(ang), (1, LANES // HALF))

    c_pad = jnp.pad(c, ((0, SUBLANES - bsz % SUBLANES), (0, 0)))
    gf = g_final.reshape(1, d)
    b_ada3 = b_ada.reshape(depth, 1, -1)
    g1, gm, g2 = (t.reshape(depth, 1, d) for t in (g_ffn1, g_mix, g_ffn2))

    for l in range(depth):
        mods, (wgu1, wd1) = _ada(
            c_pad, w_ada, b_ada3, l,
            [_grid_cast(w1_gu, l, ADA_STEPS), _grid_cast(w1_down, l, ADA_STEPS)])
        mods = mods[:bsz].reshape(bsz, N_MOD, d)

        x, (win, wqk, wcp, wap, wout) = _ffn(
            x, mods, g1, l, wgu1, wd1, gf,
            [_non_qk_cast(w_in, l, ffn_steps), _qk_cast(w_in, l, ffn_steps),
             _grid_cast(w_conv_proj, l, ffn_steps), _attn_proj_cast(w_attn_proj, l, ffn_steps),
             _grid_cast(w_out, l, ffn_steps)],
            mod_base=0, final=False)

        x, (wgu2, wd2) = _mix(
            x, mods, gm, l, cos_t, sin_t, conv_w, win, wqk, wcp, wap, wout, sinks,
            [_grid_cast(w2_gu, l, mix_steps), _grid_cast(w2_down, l, mix_steps)])

        x, _ = _ffn(x, mods, g2, l, wgu2, wd2, gf, [], mod_base=6, final=(l == depth - 1))
    return x
```

```python
import functools
from typing import Callable, NamedTuple

import jax
import jax.numpy as jnp
from jax import lax
from jax.experimental import pallas as pl
from jax.experimental.pallas import tpu as pltpu

D_MODEL = 1024
CONV_K = 3
HEAD_DIM = 64
HALF = HEAD_DIM // 2
N_HEADS = 16
N_KV_HEADS = 4
GROUP = N_HEADS // N_KV_HEADS
D_ATTN = N_HEADS * HEAD_DIM
D_KV = N_KV_HEADS * HEAD_DIM
WINDOW = 128
BLOCK = 128
QH = WINDOW // 2
ROPE_THETA = 10000.0
D_FF = 2816
N_MOD = 9
EPS = 1e-6
NEG_INF = -1e30

LANES = 128
SUBLANES = 8
BF16_SUBLANES = 16
VMEM_BYTES = 64 * 1024 * 1024
VMEM_LIMIT_BYTES = VMEM_BYTES - 4 * 1024 * 1024

TM_FFN = 1024
FFN_SUBTILES = 8
TM_MIX = 1024
MIX_SUB = 512
ADA_STEPS = 8

OFF_B, OFF_C, OFF_U = 0, D_MODEL, 2 * D_MODEL
OFF_Q = 3 * D_MODEL
OFF_K = OFF_Q + D_ATTN
OFF_V = OFF_K + D_KV
OFF_ZC = OFF_V + D_KV
OFF_ZA = OFF_ZC + D_MODEL
D_IN = OFF_ZA + D_MODEL
D_QK = D_ATTN + D_KV


def _compact(off):
    return off if off < OFF_Q else off - D_QK

BF16 = jnp.bfloat16
F32 = jnp.float32


def _const_spec(shape):
    nd = len(shape)
    return pl.BlockSpec(shape, lambda *_: (0,) * nd, pipeline_mode=pl.Buffered(1))


def _layer_spec(shape, layer):
    nd = len(shape)
    return pl.BlockSpec((None,) + tuple(shape), lambda *_: (layer,) + (0,) * nd,
                        pipeline_mode=pl.Buffered(1))


class _Cast(NamedTuple):
    src: jax.Array
    layer: int
    block: tuple
    in_index: Callable
    out_index: Callable
    out_cols: int
    pick: Callable


def _grid_cast(src, layer, steps):
    rows, cols = src.shape[1:]
    col_blocks = 1
    while (rows * col_blocks) % (steps * BF16_SUBLANES):
        col_blocks *= 2
    row_blocks = steps // col_blocks
    idx = lambda t: (t // col_blocks, t % col_blocks)
    return _Cast(src, layer, (rows // row_blocks, cols // col_blocks), idx, idx, cols,
                 lambda blk: blk)


def _cast_specs(casts, linear_step):
    in_specs, out_specs, out_shapes = [], [], []
    for cst in casts:
        col_blocks = cst.src.shape[2] // cst.block[1]
        in_specs.append(pl.BlockSpec(
            (None,) + cst.block,
            lambda *idx, cst=cst: (cst.layer,) + cst.in_index(linear_step(*idx))))
        out_specs.append(pl.BlockSpec(
            (cst.block[0], cst.out_cols // col_blocks),
            lambda *idx, cst=cst: cst.out_index(linear_step(*idx))))
        out_shapes.append(jax.ShapeDtypeStruct((cst.src.shape[1], cst.out_cols), BF16))
    return in_specs, out_specs, out_shapes


def _run_casts(picks, src_refs, dst_refs):
    for pick, src, dst in zip(picks, src_refs, dst_refs):
        dst[...] = pick(src[...]).astype(BF16)


def _rms_mod(x, g, shift, scale):
    ms = jnp.mean(x * x, axis=-1, keepdims=True)
    y = x * lax.rsqrt(ms + EPS) * g
    return y * (1.0 + scale) + shift


def _ada_kernel(*refs, picks):
    n_cast = len(picks)
    c_ref, w_ref, b_ref = refs[:3]
    cast_in = refs[3:3 + n_cast]
    o_ref = refs[3 + n_cast]
    cast_out = refs[4 + n_cast:]
    c = c_ref[...]
    ca = (c * jax.nn.sigmoid(c)).astype(BF16)
    o_ref[...] = jnp.dot(ca, w_ref[...].astype(BF16),
                         preferred_element_type=F32) + b_ref[...]
    _run_casts(picks, cast_in, cast_out)


def _ada(c_pad, w_ada, b_ada, layer, casts):
    rows = c_pad.shape[0]
    n = w_ada.shape[2]
    tn = n // ADA_STEPS
    cin, cout, cshape = _cast_specs(casts, lambda j: j)
    outs = pl.pallas_call(
        functools.partial(_ada_kernel, picks=tuple(cst.pick for cst in casts)),
        grid=(ADA_STEPS,),
        in_specs=[
            pl.BlockSpec((rows, D_MODEL), lambda j: (0, 0)),
            pl.BlockSpec((None, D_MODEL, tn), lambda j: (layer, 0, j)),
            pl.BlockSpec((None, 1, tn), lambda j: (layer, 0, j)),
        ] + cin,
        out_specs=[pl.BlockSpec((rows, tn), lambda j: (0, j))] + cout,
        out_shape=[jax.ShapeDtypeStruct((rows, n), F32)] + cshape,
        compiler_params=pltpu.CompilerParams(
            dimension_semantics=("arbitrary",),
            vmem_limit_bytes=VMEM_LIMIT_BYTES),
        name="ada_mod",
    )(c_pad, w_ada, b_ada, *[cst.src for cst in casts])
    return outs[0], outs[1:]


def _ffn_kernel(*refs, mod_base, final, picks):
    n_cast = len(picks)
    x_ref, mod_ref, g_ref, wgu_ref, wd_ref, gf_ref = refs[:6]
    cast_in = refs[6:6 + n_cast]
    o_ref = refs[6 + n_cast]
    cast_out = refs[7 + n_cast:]
    shift = mod_ref[mod_base:mod_base + 1, :]
    scale = mod_ref[mod_base + 1:mod_base + 2, :]
    gate = mod_ref[mod_base + 2:mod_base + 3, :]
    rows = x_ref.shape[0] // FFN_SUBTILES
    for r in range(FFN_SUBTILES):
        x = x_ref[r * rows:(r + 1) * rows, :]
        h = _rms_mod(x, g_ref[...], shift, scale).astype(BF16)
        gu = jnp.dot(h, wgu_ref[...], preferred_element_type=F32)
        a = gu[:, :D_FF]
        b = gu[:, D_FF:]
        act = (a * jax.nn.sigmoid(a) * b).astype(BF16)
        f = jnp.dot(act, wd_ref[...], preferred_element_type=F32)
        xn = x + (0.5 * gate) * f
        if final:
            ms = jnp.mean(xn * xn, axis=-1, keepdims=True)
            xn = xn * lax.rsqrt(ms + EPS) * gf_ref[...]
        o_ref[r * rows:(r + 1) * rows, :] = xn
    _run_casts(picks, cast_in, cast_out)


def _ffn(x, mods, g, layer, wgu, wd, g_final, casts, *, mod_base, final):
    bsz, seq, d = x.shape
    n_s = seq // TM_FFN
    cin, cout, cshape = _cast_specs(casts, lambda b, s: b * n_s + s)
    kern = functools.partial(_ffn_kernel, mod_base=mod_base, final=final,
                             picks=tuple(cst.pick for cst in casts))
    outs = pl.pallas_call(
        kern,
        grid=(bsz, n_s),
        in_specs=[
            pl.BlockSpec((None, TM_FFN, d), lambda b, s: (b, s, 0)),
            pl.BlockSpec((None, N_MOD, d), lambda b, s: (b, 0, 0)),
            _layer_spec((1, d), layer),
            _const_spec(wgu.shape),
            _const_spec(wd.shape),
            _const_spec((1, d)),
        ] + cin,
        out_specs=[pl.BlockSpec((None, TM_FFN, d), lambda b, s: (b, s, 0))] + cout,
        out_shape=[jax.ShapeDtypeStruct(x.shape, F32)] + cshape,
        compiler_params=pltpu.CompilerParams(
            dimension_semantics=("arbitrary", "arbitrary"),
            vmem_limit_bytes=VMEM_LIMIT_BYTES),
        name="ffn_final" if final else "ffn",
    )(x, mods, g, wgu, wd, g_final, *[cst.src for cst in casts])
    return outs[0], outs[1:]


def _mix_kernel(*refs, layer, picks):
    n_cast = len(picks)
    (sinks_ref, x_ref, mod_ref, g_ref, cos_ref, sin_ref, convw_ref,
     win_ref, wqk_ref, wcp_ref, wap_ref, wout_ref) = refs[:12]
    cast_in = refs[12:12 + n_cast]
    o_ref = refs[12 + n_cast]
    cast_out = refs[13 + n_cast:13 + 2 * n_cast]
    cu_scr, k_scr, vt_scr = refs[13 + 2 * n_cast:]
    tm = x_ref.shape[0]
    nblk = tm // BLOCK
    s_idx = pl.program_id(1)

    @pl.when(s_idx == 0)
    def _():
        cu_scr[0:SUBLANES, :] = jnp.zeros((SUBLANES, D_MODEL), F32)
        k_scr[:, 0:BLOCK, :] = jnp.zeros((N_KV_HEADS, BLOCK, D_KV), BF16)
        vt_scr[:, :, 0:BLOCK] = jnp.zeros((N_KV_HEADS, D_KV, BLOCK), BF16)

    @pl.when(s_idx > 0)
    def _():
        cu_scr[0:SUBLANES, :] = cu_scr[tm:tm + SUBLANES, :]
        for g in range(N_KV_HEADS):
            k_scr[g, 0:BLOCK, :] = k_scr[g, tm:tm + BLOCK, :]
            vt_scr[g, :, 0:BLOCK] = vt_scr[g, :, tm:tm + BLOCK]

    shift = mod_ref[3:4, :]
    scale = mod_ref[4:5, :]
    gate = mod_ref[5:6, :]
    lane = lax.broadcasted_iota(jnp.int32, (1, D_KV), 1)
    head_of_lane = (lane % LANES) // HALF
    vrow = lax.broadcasted_iota(jnp.int32, (D_KV, 1), 0)
    head_of_vrow = vrow // HEAD_DIM
    qscale = HEAD_DIM ** -0.5

    key_slot = lax.broadcasted_iota(jnp.int32, (QH, GROUP * QH), 0)
    query = lax.broadcasted_iota(jnp.int32, (QH, GROUP * QH), 1)
    from_a = key_slot > (query % QH)
    neg_first = jnp.where(s_idx == 0, NEG_INF, 0.0)
    sink_rows = [
        jnp.concatenate([jnp.full((1, QH), sinks_ref[layer, g * GROUP + j], F32)
                         for j in range(GROUP)], axis=1)
        for g in range(N_KV_HEADS)]

    def attend(q_all, klo):
        a_in_halo, b_in_halo = klo < BLOCK, klo + QH < BLOCK
        k_all = jnp.concatenate(
            [k_scr[g, klo:klo + 3 * QH, :] for g in range(N_KV_HEADS)], axis=0)
        st_all = lax.dot_general(k_all, q_all, (((1,), (1,)), ((), ())),
                                 preferred_element_type=F32)
        pt_parts, inv_den = [], []
        for g in range(N_KV_HEADS):
            s_a =st_all[3 * g * QH:(3 * g + 1) * QH, :]
            s_b = st_all[(3 * g + 1) * QH:(3 * g + 2) * QH, :]
            s_c = st_all[(3 * g + 2) * QH:(3 * g + 3) * QH, :]
            if a_in_halo:
                s_a = s_a + neg_first
            if b_in_halo:
                s_b = s_b + neg_first
            s = jnp.concatenate([jnp.where(from_a, s_a, s_c), s_b], axis=0)
            sink = sink_rows[g]
            m = jnp.maximum(jnp.max(s, axis=0, keepdims=True), sink)
            p = jnp.exp(s - m)
            den = jnp.sum(p, axis=0, keepdims=True) + jnp.exp(sink - m)
            inv_den.append(jnp.broadcast_to(1.0 / den, (HEAD_DIM, GROUP * QH)))
            p_ac, p_b = p[:QH, :], p[QH:, :]
            pt_parts.append(jnp.where(from_a, p_ac, 0.0).astype(BF16))
            pt_parts.append(p_b.astype(BF16))
            pt_parts.append(jnp.where(from_a, 0.0, p_ac).astype(BF16))
        pt_all = jnp.concatenate(pt_parts, axis=0)
        vt_all = jnp.concatenate(
            [vt_scr[g, :, klo:klo + 3 * QH] for g in range(N_KV_HEADS)], axis=1)
        ot_all = jnp.dot(vt_all, pt_all, preferred_element_type=F32)
        o_all = (ot_all * jnp.concatenate(inv_den, axis=0)).T
        return jnp.concatenate([o_all[j * QH:(j + 1) * QH, :] for j in range(GROUP)],
                               axis=1).astype(BF16)

    sub = min(MIX_SUB, tm)

    def proj(h, off, width):
        lo_col = _compact(off)
        return jnp.dot(h, win_ref[:, lo_col:lo_col + width], preferred_element_type=F32)

    def prepare(t):
        lo = t * sub
        x = x_ref[lo:lo + sub, :]
        h = _rms_mod(x, g_ref[...], shift, scale).astype(BF16)
        cos = cos_ref[lo:lo + sub, :]
        sin = sin_ref[lo:lo + sub, :]

        k = jnp.dot(h, wqk_ref[:, D_ATTN:], preferred_element_type=F32)
        k1, k2 = k[:, :LANES], k[:, LANES:]
        kr = jnp.concatenate([k1 * cos - k2 * sin, k2 * cos + k1 * sin], axis=-1)
        vt = proj(h, OFF_V, D_KV).T
        for g in range(N_KV_HEADS):
            k_scr[g, BLOCK + lo:BLOCK + lo + sub, :] = (
                jnp.where(head_of_lane == g, kr, 0.0).astype(BF16))
            vt_scr[g, :, BLOCK + lo:BLOCK + lo + sub] = (
                jnp.where(head_of_vrow == g, vt, 0.0).astype(BF16))

        q = jnp.dot(h, wqk_ref[:, :D_ATTN], preferred_element_type=F32)
        q_rot = []
        for j in range(GROUP):
            q1 = q[:, j * D_KV:j * D_KV + LANES]
            q2 = q[:, j * D_KV + LANES:(j + 1) * D_KV]
            q_rot.append(jnp.concatenate([(q1 * cos - q2 * sin) * qscale,
                                          (q2 * cos + q1 * sin) * qscale],
                                         axis=-1).astype(BF16))
        return lo, x, h, q_rot

    def attention(lo, x, h, q_rot):
        other = [OFF_C, OFF_U, OFF_B, OFF_ZC, OFF_ZA]
        n_groups = sub // QH
        projs, attn_blocks = {}, []
        for i in range(n_groups):
            q_all = jnp.concatenate([qr[i * QH:(i + 1) * QH, :] for qr in q_rot], axis=0)
            attn_blocks.append(attend(q_all, lo + i * QH))
            for off in other[i::n_groups]:
                projs[off] = proj(h, off, D_MODEL)
        return lo, x, projs, jnp.concatenate(attn_blocks, axis=0)

    def finish(lo, x, projs, attn):
        y_attn = jnp.dot(attn, wap_ref[...], preferred_element_type=F32)
        cu = projs[OFF_C] * projs[OFF_U]
        cu_scr[SUBLANES + lo:SUBLANES + lo + sub, :] = cu
        conv = (cu_scr[SUBLANES - 2 + lo:SUBLANES - 2 + lo + sub, :] * convw_ref[0:1, :]
                + cu_scr[SUBLANES - 1 + lo:SUBLANES - 1 + lo + sub, :] * convw_ref[1:2, :]
                + cu * convw_ref[2:3, :])
        bconv = (projs[OFF_B] * conv).astype(BF16)
        y_conv = jnp.dot(bconv, wcp_ref[...], preferred_element_type=F32)

        merged = (jax.nn.sigmoid(projs[OFF_ZC]) * y_conv
                  + jax.nn.sigmoid(projs[OFF_ZA]) * y_attn).astype(BF16)
        o_ref[lo:lo + sub, :] = x + gate * jnp.dot(merged, wout_ref[...],
                                                   preferred_element_type=F32)

    n_sub = tm // sub
    prepared = prepare(0)
    for t in range(n_sub):
        attended = attention(*prepared)
        if t + 1 < n_sub:
            prepared = prepare(t + 1)
        finish(*attended)
    _run_casts(picks, cast_in, cast_out)


def _mix(x, mods, g, layer, cos_t, sin_t, conv_w, w_in, wqk, wcp, wap, wout, sinks, casts):
    bsz, seq, d = x.shape
    tm = TM_MIX
    nblk = tm // BLOCK
    n_s = seq // tm
    cin, cout, cshape = _cast_specs(casts, lambda b, s: b * n_s + s)
    outs = pl.pallas_call(
        functools.partial(_mix_kernel, layer=layer, picks=tuple(cst.pick for cst in casts)),
        grid=(bsz, n_s),
        in_specs=[
            pl.BlockSpec(memory_space=pltpu.SMEM),
            pl.BlockSpec((None, tm, d), lambda b, s: (b, s, 0)),
            pl.BlockSpec((None, N_MOD, d), lambda b, s: (b, 0, 0)),
            _layer_spec((1, d), layer),
            pl.BlockSpec((tm, LANES), lambda b, s: (s, 0)),
            pl.BlockSpec((tm, LANES), lambda b, s: (s, 0)),
            _layer_spec((CONV_K, d), layer),
            _const_spec(w_in.shape),
            _const_spec(wqk.shape),
            _const_spec(wcp.shape),
            _const_spec(wap.shape),
            _const_spec(wout.shape),
        ] + cin,
        out_specs=[pl.BlockSpec((None, tm, d), lambda b, s: (b, s, 0))] + cout,
        scratch_shapes=[
            pltpu.VMEM((SUBLANES + tm, d), F32),
            pltpu.VMEM((N_KV_HEADS, BLOCK + tm, D_KV), BF16),
            pltpu.VMEM((N_KV_HEADS, D_KV, BLOCK + tm), BF16),
        ],
        out_shape=[jax.ShapeDtypeStruct(x.shape, F32)] + cshape,
        compiler_params=pltpu.CompilerParams(
            dimension_semantics=("arbitrary", "arbitrary"),
            vmem_limit_bytes=VMEM_LIMIT_BYTES),
        name="mixer",
    )(sinks, x, mods, g, cos_t, sin_t, conv_w, w_in, wqk, wcp, wap, wout,
      *[cst.src for cst in casts])
    return outs[0], outs[1:]


def _pick_permuted_qk(w_rows):
    pieces = []
    for j in range(GROUP):
        for half in range(2):
            for g in range(N_KV_HEADS):
                start = OFF_Q + (g * GROUP + j) * HEAD_DIM + half * HALF
                pieces.append(w_rows[:, start:start + HALF])
    for half in range(2):
        for g in range(N_KV_HEADS):
            start = OFF_K + g * HEAD_DIM + half * HALF
            pieces.append(w_rows[:, start:start + HALF])
    return jnp.concatenate(pieces, axis=1)


def _qk_cast(w_in, layer, steps):
    rows, cols = w_in.shape[1:]
    idx = lambda t: (t, 0)
    return _Cast(w_in, layer, (rows // steps, cols), idx, idx, D_QK, _pick_permuted_qk)


def _non_qk_cast(w_in, layer, steps):
    rows, cols = w_in.shape[1:]
    idx = lambda t: (t, 0)
    pick = lambda w_rows: jnp.concatenate([w_rows[:, :OFF_Q], w_rows[:, OFF_V:]], axis=1)
    return _Cast(w_in, layer, (rows // steps, cols), idx, idx, D_IN - D_QK, pick)


def _attn_proj_cast(w_attn_proj, layer, steps):
    rows, cols = w_attn_proj.shape[1:]
    blk = rows // steps
    per_head = HEAD_DIM // blk

    def in_index(t):
        chunk, within = t // per_head, t % per_head
        return ((chunk % N_KV_HEADS) * GROUP + chunk // N_KV_HEADS) * per_head + within, 0

    return _Cast(w_attn_proj, layer, (blk, cols), in_index, lambda t: (t, 0), cols,
                 lambda blk_rows: blk_rows)


def kernel(x, c, w_ada, b_ada, g_ffn1, w1_gu, w1_down, g_mix, w_in, conv_w,
           w_conv_proj, w_attn_proj, sinks, w_out, g_ffn2, w2_gu, w2_down, g_final):
    bsz, seq, d = x.shape
    depth = w_ada.shape[0]
    ffn_steps = bsz * (seq // TM_FFN)
    mix_steps = bsz * (seq // TM_MIX)

    inv = 1.0 / (ROPE_THETA ** (jnp.arange(0, HEAD_DIM, 2, dtype=F32) / HEAD_DIM))
    ang = jnp.arange(seq, dtype=F32)[:, None] * inv[None, :]
    cos_t = jnp.tile(jnp.cos(ang), (1, LANES // HALF))
    sin_t = jnp.tile(jnp.sin(ang), (1, LANES // HALF))

    c_pad = jnp.pad(c, ((0, SUBLANES - bsz % SUBLANES), (0, 0)))
    gf = g_final.reshape(1, d)
    b_ada3 = b_ada.reshape(depth, 1, -1)
    g1, gm, g2 = (t.reshape(depth, 1, d) for t in (g_ffn1, g_mix, g_ffn2))

    for l in range(depth):
        mods, (wgu1, wd1) = _ada(
            c_pad, w_ada, b_ada3, l,
            [_grid_cast(w1_gu, l, ADA_STEPS), _grid_cast(w1_down, l, ADA_STEPS)])
        mods = mods[:bsz].reshape(bsz, N_MOD, d)

        x, (win, wqk, wcp, wap, wout) = _ffn(
            x, mods, g1, l, wgu1, wd1, gf,
            [_non_qk_cast(w_in, l, ffn_steps), _qk_cast(w_in, l, ffn_steps),
             _grid_cast(w_conv_proj, l, ffn_steps), _attn_proj_cast(w_attn_proj, l, ffn_steps),
             _grid_cast(w_out, l, ffn_steps)],
            mod_base=0, final=False)

        x, (wgu2, wd2) = _mix(
            x, mods, gm, l, cos_t, sin_t, conv_w, win, wqk, wcp, wap, wout, sinks,
            [_grid_cast(w2_gu, l, mix_steps), _grid_cast(w2_down, l, mix_steps)])

        x, _ = _ffn(x, mods, g2, l, wgu2, wd2, gf, [], mod_base=6, final=(l == depth - 1))
    return x
```
